```python
import jax, jax.numpy as jnp
from jax import lax
import numpy as np

D_MODEL = 1024
BATCH = 8
SEQ = 8192
DEPTH = 1
DEC_BATCH = 32
DEC_SEQ = 16
PAST_LEN = 4096

CHUNK = 64
LEFT_CHUNKS = 8
HEAD_DIM = 64
H_A = 4
H_B = 8
H_M = 4
N_MEM = 256
REL_CLIP = 256
SB_BLOCK = 128
D_FF = 2816
CONV_W = 3
W_A = H_A * HEAD_DIM
W_B = H_B * HEAD_DIM
W_M = H_M * HEAD_DIM
D_IN = 3 * W_A + 3 * W_B + W_M
IN_SPLITS = [W_A, 2 * W_A, 3 * W_A, 3 * W_A + W_B, 3 * W_A + 2 * W_B, 3 * W_A + 3 * W_B]
DN_ALPHA = (2 * DEPTH) ** 0.25
DN_BETA = (8 * DEPTH) ** -0.25
LN_EPS = 1e-5
NEG_INF = -1e30

kernel_name = "chunk_stream_hybrid_encoder_step"


def layer_norm(x, g, b):
    xf = x.astype(jnp.float32)
    mu = jnp.mean(xf, axis=-1, keepdims=True)
    var = jnp.mean(jnp.square(xf - mu), axis=-1, keepdims=True)
    y = (xf - mu) * lax.rsqrt(var + LN_EPS) * g.astype(jnp.float32) + b.astype(jnp.float32)
    return y.astype(x.dtype)


def heads(t, h):
    return t.reshape(t.shape[:-1] + (h, HEAD_DIM))


def in_projection(x, w_in):
    qa, ka, va, qb, kb, vb, qm = jnp.split(x @ w_in, IN_SPLITS, axis=-1)
    return (heads(qa, H_A), heads(ka, H_A), heads(va, H_A),
            heads(qb, H_B), heads(kb, H_B), heads(vb, H_B), heads(qm, H_M))


def memory_kv(mem, w_mem_kv):
    mk, mv = jnp.split(mem @ w_mem_kv, 2, axis=-1)
    return heads(mk, H_M), heads(mv, H_M)


def rel_bias_lookup(table, rel):
    idx = jnp.clip(rel, -REL_CLIP, REL_CLIP) + REL_CLIP
    return table[:, idx].astype(jnp.float32)


def chunk_band_attention(q, k, v, bias):
    s = jnp.einsum('bnqhd,bnkhd->bnhqk', q, k).astype(jnp.float32) * (HEAD_DIM ** -0.5) + bias[None]
    p = jax.nn.softmax(s, axis=-1).astype(v.dtype)
    return jnp.einsum('bnhqk,bnkhd->bnqhd', p, v)


def chunk_attention_prompt(q, k, v, table):
    B, S = q.shape[0], q.shape[1]
    nc = S // CHUNK
    band = (LEFT_CHUNKS + 1) * CHUNK
    qc = q.reshape(B, nc, CHUNK, H_A, HEAD_DIM)

    def gather_band(t):
        tc = t.reshape(B, nc, CHUNK, H_A, HEAD_DIM)
        tp = jnp.pad(tc, ((0, 0), (LEFT_CHUNKS, 0), (0, 0), (0, 0), (0, 0)))
        return jnp.concatenate([tp[:, j:j + nc] for j in range(LEFT_CHUNKS + 1)], axis=2)

    kb, vb = gather_band(k), gather_band(v)
    rel = LEFT_CHUNKS * CHUNK + jnp.arange(CHUNK)[:, None] - jnp.arange(band)[None, :]
    bias = rel_bias_lookup(table, rel)
    key_chunk = jnp.arange(nc)[:, None] - LEFT_CHUNKS + jnp.arange(band)[None, :] // CHUNK
    mask_add = jnp.where(key_chunk >= 0, 0.0, NEG_INF).astype(jnp.float32)
    o = chunk_band_attention(qc, kb, vb, bias[None] + mask_add[:, None, None, :])
    return o.reshape(B, S, W_A)


def chunk_attention_sample(q, k_new, v_new, ck, cv, table, past_len):
    B, T = q.shape[0], q.shape[1]
    C = ck.shape[1]
    kk = jnp.concatenate([ck, k_new], axis=1)
    vv = jnp.concatenate([cv, v_new], axis=1)
    q_pos = past_len + jnp.arange(T)
    k_pos = jnp.concatenate([past_len - C + jnp.arange(C), past_len + jnp.arange(T)])
    bias = rel_bias_lookup(table, q_pos[:, None] - k_pos[None, :])[None]
    o = chunk_band_attention(q[:, None], kk[:, None], vv[:, None], bias)
    return o.reshape(B, T, W_A)


def stick_breaking_attention(q, k, v, q_pos, k_pos):
    z = jnp.einsum('bqhd,bkhd->bhqk', q, k).astype(jnp.float32) * (HEAD_DIM ** -0.5)
    mask = (k_pos[None, :] < q_pos[:, None])[None, None]
    log_stay = jnp.where(mask, jax.nn.log_sigmoid(-z), 0.0)
    later = lax.cumsum(log_stay, axis=3, reverse=True) - log_stay
    w = jnp.where(mask, jnp.exp(jax.nn.log_sigmoid(z) + later), 0.0)
    return jnp.einsum('bhqk,bkhd->bqhd', w.astype(v.dtype), v)


def stick_breaking_prompt(q, k, v):
    B, S = q.shape[0], q.shape[1]
    nb = S // SB_BLOCK
    pos = jnp.arange(S, dtype=jnp.int32)
    qb = q.reshape(B, nb, SB_BLOCK, H_B, HEAD_DIM).transpose(1, 0, 2, 3, 4)
    pb = pos.reshape(nb, SB_BLOCK)
    o = lax.map(lambda a: stick_breaking_attention(a[0], k, v, a[1], pos), (qb, pb))
    return o.transpose(1, 0, 2, 3, 4).reshape(B, S, W_B)


def stick_breaking_sample(q, k_new, v_new, ck, cv, past_len):
    T = q.shape[1]
    kk = jnp.concatenate([ck, k_new], axis=1)
    vv = jnp.concatenate([cv, v_new], axis=1)
    k_pos = jnp.arange(past_len + T, dtype=jnp.int32)
    q_pos = past_len + jnp.arange(T, dtype=jnp.int32)
    o = stick_breaking_attention(q, kk, vv, q_pos, k_pos)
    return o.reshape(q.shape[0], T, W_B)


def memory_attention(q, mk, mv):
    s = jnp.einsum('bqhd,bmhd->bhqm', q, mk).astype(jnp.float32) * (HEAD_DIM ** -0.5)
    p = jax.nn.softmax(s, axis=-1).astype(mv.dtype)
    o = jnp.einsum('bhqm,bmhd->bqhd', p, mv)
    return o.reshape(q.shape[0], q.shape[1], W_M)


def merge_branches(x, oa, ob, om, w_pa, w_pb, w_pm, w_gate, b_gate, w_o):
    g = jax.nn.sigmoid((x @ w_gate + b_gate).astype(jnp.float32)).astype(x.dtype)
    ga, gb, gm = jnp.split(g, 3, axis=-1)
    h = ga * (oa @ w_pa) + gb * (ob @ w_pb) + gm * (om @ w_pm)
    return h @ w_o


def conv_ffn_tail(ext, conv_w, conv_b, w_down):
    T = ext.shape[1] - (CONV_W - 1)
    c = sum(ext[:, j:j + T] * conv_w[j] for j in range(CONV_W)) + conv_b
    gate, val = jnp.split(c, 2, axis=-1)
    return (jax.nn.silu(gate) * val) @ w_down


def setup_inputs(seed: int = 0) -> dict:
    key = jax.random.key(seed)
    ks = jax.random.split(key, 32)
    nrm = jax.random.normal
    f32 = jnp.float32
    a_cache = min(LEFT_CHUNKS * CHUNK, PAST_LEN)
    return {
        "x_prompt": nrm(ks[0], (BATCH, SEQ, D_MODEL), f32),
        "x_sample": nrm(ks[1], (DEC_BATCH, DEC_SEQ, D_MODEL), f32),
        "cache_a_k": nrm(ks[2], (DEPTH, DEC_BATCH, a_cache, H_A, HEAD_DIM), f32),
        "cache_a_v": nrm(ks[3], (DEPTH, DEC_BATCH, a_cache, H_A, HEAD_DIM), f32),
        "cache_b_k": nrm(ks[4], (DEPTH, DEC_BATCH, PAST_LEN, H_B, HEAD_DIM), f32),
        "cache_b_v": nrm(ks[5], (DEPTH, DEC_BATCH, PAST_LEN, H_B, HEAD_DIM), f32),
        "cache_mem_k": nrm(ks[6], (DEPTH, DEC_BATCH, N_MEM, H_M, HEAD_DIM), f32),
        "cache_mem_v": nrm(ks[7], (DEPTH, DEC_BATCH, N_MEM, H_M, HEAD_DIM), f32),
        "state_ffn_conv": nrm(ks[8], (DEPTH, DEC_BATCH, CONV_W - 1, 2 * D_FF), f32),
        "mem_prompt": nrm(ks[9], (BATCH, N_MEM, D_MODEL), f32),
        "w_in": nrm(ks[10], (DEPTH, D_MODEL, D_IN), f32) * D_MODEL ** -0.5,
        "rel_bias": 0.1 * nrm(ks[11], (DEPTH, H_A, 2 * REL_CLIP + 1), f32),
        "w_mem_kv": nrm(ks[12], (DEPTH, D_MODEL, 2 * W_M), f32) * D_MODEL ** -0.5,
        "w_pa": nrm(ks[13], (DEPTH, W_A, D_MODEL), f32) * W_A ** -0.5,
        "w_pb": nrm(ks[14], (DEPTH, W_B, D_MODEL), f32) * W_B ** -0.5,
        "w_pm": nrm(ks[15], (DEPTH, W_M, D_MODEL), f32) * W_M ** -0.5,
        "w_gate": nrm(ks[16], (DEPTH, D_MODEL, 3 * D_MODEL), f32) * D_MODEL ** -0.5,
        "b_gate": 0.01 * nrm(ks[17], (DEPTH, 3 * D_MODEL), f32),
        "w_o": nrm(ks[18], (DEPTH, D_MODEL, D_MODEL), f32) * (D_MODEL ** -0.5 * DN_BETA),
        "ln1_g": 1.0 + 0.01 * nrm(ks[19], (DEPTH, D_MODEL), f32),
        "ln1_b": 0.01 * nrm(ks[20], (DEPTH, D_MODEL), f32),
        "w_up": nrm(ks[21], (DEPTH, D_MODEL, 2 * D_FF), f32) * D_MODEL ** -0.5,
        "conv_w": nrm(ks[22], (DEPTH, CONV_W, 2 * D_FF), f32) * CONV_W ** -0.5,
        "conv_b": 0.01 * nrm(ks[23], (DEPTH, 2 * D_FF), f32),
        "w_down": nrm(ks[24], (DEPTH, D_FF, D_MODEL), f32) * (D_FF ** -0.5 * DN_BETA),
        "ln2_g": 1.0 + 0.01 * nrm(ks[25], (DEPTH, D_MODEL), f32),
        "ln2_b": 0.01 * nrm(ks[26], (DEPTH, D_MODEL), f32),
    }


def reference(x_prompt, x_sample, cache_a_k, cache_a_v, cache_b_k, cache_b_v, cache_mem_k, cache_mem_v,
              state_ffn_conv, mem_prompt, w_in, rel_bias, w_mem_kv, w_pa, w_pb, w_pm, w_gate, b_gate, w_o,
              ln1_g, ln1_b, w_up, conv_w, conv_b, w_down, ln2_g, ln2_b):
    past_len = cache_b_k.shape[2]
    a_keep = min(LEFT_CHUNKS * CHUNK, x_prompt.shape[1])
    yp, ys = x_prompt, x_sample
    pak, pav, pbk, pbv, pmk, pmv, pconv = [], [], [], [], [], [], []
    sak, sav, sbk, sbv, sconv = [], [], [], [], []
    for l in range(DEPTH):
        qa, ka, va, qb, kb, vb, qm = in_projection(yp, w_in[l])
        mk, mv = memory_kv(mem_prompt, w_mem_kv[l])
        oa = chunk_attention_prompt(qa, ka, va, rel_bias[l])
        ob = stick_breaking_prompt(qb, kb, vb)
        om = memory_attention(qm, mk, mv)
        mix = merge_branches(yp, oa, ob, om, w_pa[l], w_pb[l], w_pm[l], w_gate[l], b_gate[l], w_o[l])
        h = layer_norm(DN_ALPHA * yp + mix, ln1_g[l], ln1_b[l])
        u = h @ w_up[l]
        ext = jnp.pad(u, ((0, 0), (CONV_W - 1, 0), (0, 0)))
        yp = layer_norm(DN_ALPHA * h + conv_ffn_tail(ext, conv_w[l], conv_b[l], w_down[l]), ln2_g[l], ln2_b[l])
        pak.append(ka[:, -a_keep:]); pav.append(va[:, -a_keep:])
        pbk.append(kb); pbv.append(vb)
        pmk.append(mk); pmv.append(mv)
        pconv.append(u[:, -(CONV_W - 1):])
        qa, ka, va, qb, kb, vb, qm = in_projection(ys, w_in[l])
        oa = chunk_attention_sample(qa, ka, va, cache_a_k[l], cache_a_v[l], rel_bias[l], past_len)
        ob = stick_breaking_sample(qb, kb, vb, cache_b_k[l], cache_b_v[l], past_len)
        om = memory_attention(qm, cache_mem_k[l], cache_mem_v[l])
        mix = merge_branches(ys, oa, ob, om, w_pa[l], w_pb[l], w_pm[l], w_gate[l], b_gate[l], w_o[l])
        h = layer_norm(DN_ALPHA * ys + mix, ln1_g[l], ln1_b[l])
        u = h @ w_up[l]
        ext = jnp.concatenate([state_ffn_conv[l].astype(u.dtype), u], axis=1)
        ys = layer_norm(DN_ALPHA * h + conv_ffn_tail(ext, conv_w[l], conv_b[l], w_down[l]), ln2_g[l], ln2_b[l])
        sak.append(ka); sav.append(va)
        sbk.append(kb); sbv.append(vb)
        sconv.append(ext[:, -(CONV_W - 1):])
    return (yp, ys,
            jnp.stack(pak), jnp.stack(pav), jnp.stack(pbk), jnp.stack(pbv),
            jnp.stack(pmk), jnp.stack(pmv), jnp.stack(pconv),
            jnp.stack(sak), jnp.stack(sav), jnp.stack(sbk), jnp.stack(sbv), jnp.stack(sconv))
```

```python
import functools

import jax
import jax.numpy as jnp
from jax import lax
from jax.experimental import pallas as pl
from jax.experimental.pallas import tpu as pltpu

F32 = jnp.float32
BF16 = jnp.bfloat16

CHUNK = 64
LEFT_CHUNKS = 8
HEAD_DIM = 64
H_A = 4
H_B = 8
H_M = 4
REL_CLIP = 256
CONV_W = 3
LN_EPS = 1e-5
NEG_INF = -1e30
W_A = H_A * HEAD_DIM
W_B = H_B * HEAD_DIM
W_M = H_M * HEAD_DIM
BAND = LEFT_CHUNKS * CHUNK
HEADS_PER_GROUP = 4
GROUP_W = HEADS_PER_GROUP * HEAD_DIM
Q_SCALE = HEAD_DIM ** -0.5

ROW_TILE = 512
PAIR = 2 * CHUNK
PAIR_WIN = BAND + PAIR
SB_TQ = 128
SB_KB = 256
FF_CHUNK = 256
HIST_ROWS = 8
VMEM_LIMIT = 56 * 1024 * 1024


def _dot(a, b):
    return jnp.dot(a, b, preferred_element_type=F32)


def _dot_nt(a, b):
    return lax.dot_general(a, b, (((1,), (1,)), ((), ())), preferred_element_type=F32)


def _head_masks():
    lane = lax.broadcasted_iota(jnp.int32, (1, GROUP_W), 1)
    return [(lane >> 6) == h for h in range(HEADS_PER_GROUP)]


def _stack_heads(q, masks):
    zero = jnp.zeros_like(q)
    return jnp.concatenate([jnp.where(m, q, zero) for m in masks], axis=0)


def _unstack_heads(pv, masks, t):
    out = jnp.where(masks[0], pv[0:t], 0.0)
    for h in range(1, HEADS_PER_GROUP):
        out = out + jnp.where(masks[h], pv[h * t:(h + 1) * t], 0.0)
    return out


def _layer_norm(x, g, b):
    mu = jnp.mean(x, axis=-1, keepdims=True)
    xc = x - mu
    var = jnp.mean(xc * xc, axis=-1, keepdims=True)
    return xc * lax.rsqrt(var + LN_EPS) * g + b


def _sigmoid(x):
    return 1.0 / (1.0 + jnp.exp(-x))


def _inproj_kernel(x_ref, w_ref, qa, ka, va, qb, kb, vb, qm, ka32, va32, kb32, vb32):
    x = x_ref[...].astype(BF16)
    o = 0

    def seg(width):
        nonlocal o
        y = _dot(x, w_ref[:, o:o + width])
        o += width
        return y

    qa[...] = (seg(W_A) * Q_SCALE).astype(BF16)
    y = seg(W_A); ka[...] = y.astype(BF16); ka32[...] = y
    y = seg(W_A); va[...] = y.astype(BF16); va32[...] = y
    qb[...] = (seg(W_B) * Q_SCALE).astype(BF16)
    y = seg(W_B); kb[...] = y.astype(BF16); kb32[...] = y
    y = seg(W_B); vb[...] = y.astype(BF16); vb32[...] = y
    qm[...] = (seg(W_M) * Q_SCALE).astype(BF16)


def _in_projection(x2d, w_in_bf):
    rows, d = x2d.shape
    tm = min(ROW_TILE, rows)
    assert rows % tm == 0
    d_in = w_in_bf.shape[1]
    widths = [W_A, W_A, W_A, W_B, W_B, W_B, W_M, W_A, W_A, W_B, W_B]
    dtypes = [BF16] * 7 + [F32] * 4
    return pl.pallas_call(
        _inproj_kernel,
        grid=(rows // tm,),
        in_specs=[pl.BlockSpec((tm, d), lambda i: (i, 0)),
                  pl.BlockSpec((d, d_in), lambda i: (0, 0))],
        out_specs=[pl.BlockSpec((tm, w), lambda i: (i, 0)) for w in widths],
        out_shape=[jax.ShapeDtypeStruct((rows, w), dt) for w, dt in zip(widths, dtypes)],
        compiler_params=pltpu.CompilerParams(
            dimension_semantics=("arbitrary",), vmem_limit_bytes=VMEM_LIMIT),
        name="in_projection",
    )(x2d, w_in_bf)


def _memkv_kernel(m_ref, w_ref, k32, v32, kbf, vbf):
    y = _dot(m_ref[...].astype(BF16), w_ref[...])
    k = y[:, :W_M]
    v = y[:, W_M:]
    k32[...] = k
    v32[...] = v
    kbf[...] = k.astype(BF16)
    vbf[...] = v.astype(BF16)


def _memory_kv(mem2d, w_bf):
    rows, d = mem2d.shape
    tm = min(ROW_TILE, rows)
    assert rows % tm == 0
    return pl.pallas_call(
        _memkv_kernel,
        grid=(rows // tm,),
        in_specs=[pl.BlockSpec((tm, d), lambda i: (i, 0)),
                  pl.BlockSpec((d, 2 * W_M), lambda i: (0, 0))],
        out_specs=[pl.BlockSpec((tm, W_M), lambda i: (i, 0))] * 4,
        out_shape=[jax.ShapeDtypeStruct((rows, W_M), F32)] * 2
        + [jax.ShapeDtypeStruct((rows, W_M), BF16)] * 2,
        compiler_params=pltpu.CompilerParams(
            dimension_semantics=("arbitrary",), vmem_limit_bytes=VMEM_LIMIT),
        name="memory_kv",
    )(mem2d, w_bf)


def _softmax_pv(score_value_pairs):
    m = None
    for s, _ in score_value_pairs:
        ms = jnp.max(s, axis=-1, keepdims=True)
        m = ms if m is None else jnp.maximum(m, ms)
    acc = None
    l = None
    for s, v in score_value_pairs:
        e = jnp.exp(s - m)
        ls = jnp.sum(e, axis=-1, keepdims=True)
        pv = _dot(e.astype(BF16), v)
        acc = pv if acc is None else acc + pv
        l = ls if l is None else l + ls
    return acc, l


def _memory_attention_tile(q, mk, mv, masks):
    t = q.shape[0]
    qs = _stack_heads(q, masks)
    s = _dot_nt(qs, mk)
    acc, l = _softmax_pv([(s, mv)])
    return _unstack_heads(acc / l, masks, t)


def _attn_am_prompt_kernel(qa_ref, kp_ref, kc_ref, vp_ref, vc_ref, qm_ref, mk_ref, mv_ref,
                           bias_ref, oa_ref, om_ref, kwin, vwin):
    t = pl.program_id(1)
    tq = qa_ref.shape[0]
    masks = _head_masks()
    kwin[0:BAND, :] = kp_ref[...]
    kwin[BAND:BAND + tq, :] = kc_ref[...]
    vwin[0:BAND, :] = vp_ref[...]
    vwin[BAND:BAND + tq, :] = vc_ref[...]
    col = lax.broadcasted_iota(jnp.int32, (1, PAIR_WIN), 1)
    for p in range(tq // PAIR):
        r0 = p * PAIR
        q = qa_ref[r0:r0 + PAIR, :]
        qs = _stack_heads(q, masks)
        s = _dot_nt(qs, kwin[r0:r0 + PAIR_WIN, :])
        s = s + bias_ref[...]
        before_start = jnp.logical_and(t == 0, col < BAND - r0)
        s = jnp.where(before_start, NEG_INF, s)
        acc, l = _softmax_pv([(s, vwin[r0:r0 + PAIR_WIN, :])])
        oa_ref[r0:r0 + PAIR, :] = _unstack_heads(acc / l, masks, PAIR).astype(BF16)
    mk = mk_ref[...]
    mv = mv_ref[...]
    for p in range(tq // PAIR):
        r0 = p * PAIR
        o = _memory_attention_tile(qm_ref[r0:r0 + PAIR, :], mk, mv, masks)
        om_ref[r0:r0 + PAIR, :] = o.astype(BF16)


def _pair_bias(table):
    i = jnp.arange(PAIR)[:, None]
    j = jnp.arange(PAIR_WIN)[None, :]
    rel = BAND + i - j
    idx = jnp.clip(rel, -REL_CLIP, REL_CLIP) + REL_CLIP
    qc = i // CHUNK
    kc = j // CHUNK
    visible = jnp.logical_and(kc >= qc, kc <= qc + LEFT_CHUNKS)
    bias = jnp.where(visible[None], table[:, idx].astype(F32), NEG_INF)
    return bias.reshape(H_A * PAIR, PAIR_WIN)


def _attn_am_prompt(qa, ka, va, qm, mk, mv, bias, batch, seq, n_mem):
    tq = BAND
    assert seq % tq == 0
    nt = seq // tq
    cur = lambda b, t: (b * nt + t, 0)
    prev = lambda b, t: (b * nt + jnp.maximum(t - 1, 0), 0)
    tile = lambda im: pl.BlockSpec((tq, GROUP_W), im)
    return pl.pallas_call(
        _attn_am_prompt_kernel,
        grid=(batch, nt),
        in_specs=[tile(cur), tile(prev), tile(cur), tile(prev), tile(cur), tile(cur),
                  pl.BlockSpec((n_mem, GROUP_W), lambda b, t: (b, 0)),
                  pl.BlockSpec((n_mem, GROUP_W), lambda b, t: (b, 0)),
                  pl.BlockSpec(bias.shape, lambda b, t: (0, 0))],
        out_specs=[tile(cur), tile(cur)],
        out_shape=[jax.ShapeDtypeStruct((batch * seq, GROUP_W), BF16)] * 2,
        scratch_shapes=[pltpu.VMEM((BAND + tq, GROUP_W), BF16)] * 2,
        compiler_params=pltpu.CompilerParams(
            dimension_semantics=("arbitrary", "arbitrary"), vmem_limit_bytes=VMEM_LIMIT),
        name="attn_am_prompt",
    )(qa, ka, ka, va, va, qm, mk, mv, bias)


def _sb_block(qs, k_blk, v_blk, tmat, carry, mask):
    z = _dot_nt(qs, k_blk)
    log_stay = -(jnp.maximum(z, 0.0) + jnp.log(1.0 + jnp.exp(-jnp.abs(z))))
    if mask is not None:
        log_stay = jnp.where(mask, log_stay, 0.0)
    hi = log_stay.astype(BF16)
    lo = (log_stay - hi.astype(F32)).astype(BF16)
    incl = _dot(hi, tmat) + _dot(lo, tmat)
    w = jnp.exp(z + incl + carry)
    if mask is not None:
        w = jnp.where(mask, w, 0.0)
    return _dot(w.astype(BF16), v_blk), carry + incl[:, 0:1]


def _sb_prompt_kernel(q_ref, k_ref, v_ref, t_ref, o_ref, acc_ref, carry_ref):
    i = pl.program_id(2)
    masks = _head_masks()
    tmat = t_ref[...]
    qs = _stack_heads(q_ref[...], masks)
    rows = qs.shape[0]
    jd = (i * SB_TQ) // SB_KB

    q_pos = i * SB_TQ + (lax.broadcasted_iota(jnp.int32, (rows, 1), 0) & (SB_TQ - 1))
    k_pos = jd * SB_KB + lax.broadcasted_iota(jnp.int32, (1, SB_KB), 1)
    mask = k_pos < q_pos
    k0 = pl.multiple_of(jd * SB_KB, SB_KB)
    pv, carry = _sb_block(qs, k_ref[pl.ds(k0, SB_KB), :], v_ref[pl.ds(k0, SB_KB), :], tmat,
                          jnp.zeros((rows, 1), F32), mask)
    acc_ref[...] = pv
    carry_ref[...] = carry

    def body(it, _):
        j = jd - 1 - it
        kk = pl.multiple_of(j * SB_KB, SB_KB)
        pv, carry = _sb_block(qs, k_ref[pl.ds(kk, SB_KB), :], v_ref[pl.ds(kk, SB_KB), :], tmat,
                              carry_ref[...], None)
        acc_ref[...] += pv
        carry_ref[...] = carry
        return 0

    lax.fori_loop(0, jd, body, 0)
    o_ref[...] = _unstack_heads(acc_ref[...], masks, SB_TQ).astype(BF16)


def _cumsum_matrix(n):
    j = jnp.arange(n)[:, None]
    s = jnp.arange(n)[None, :]
    return (j >= s).astype(BF16)


def _sb_prompt(qb, kb, vb, batch, seq):
    assert seq % SB_KB == 0 and SB_KB % SB_TQ == 0
    nq = seq // SB_TQ
    groups = W_B // GROUP_W
    rows = HEADS_PER_GROUP * SB_TQ
    return pl.pallas_call(
        _sb_prompt_kernel,
        grid=(batch, groups, nq),
        in_specs=[pl.BlockSpec((SB_TQ, GROUP_W), lambda b, g, i: (b * nq + i, g)),
                  pl.BlockSpec((seq, GROUP_W), lambda b, g, i: (b, g)),
                  pl.BlockSpec((seq, GROUP_W), lambda b, g, i: (b, g)),
                  pl.BlockSpec((SB_KB, SB_KB), lambda b, g, i: (0, 0))],
        out_specs=pl.BlockSpec((SB_TQ, GROUP_W), lambda b, g, i: (b * nq + i, g)),
        out_shape=jax.ShapeDtypeStruct((batch * seq, W_B), BF16),
        scratch_shapes=[pltpu.VMEM((rows, GROUP_W), F32), pltpu.VMEM((rows, 1), F32)],
        compiler_params=pltpu.CompilerParams(
            dimension_semantics=("arbitrary", "arbitrary", "arbitrary"), vmem_limit_bytes=VMEM_LIMIT),
        name="stick_breaking_prompt",
    )(qb, kb, vb, _cumsum_matrix(SB_KB))


NEW_PAD = 128


def _attn_am_sample_kernel(qa_ref, kn_ref, vn_ref, ck_ref, cv_ref, qm_ref, mk_ref, mv_ref,
                           bias_c_ref, bias_n_ref, oa_ref, om_ref):
    masks = _head_masks()
    t = qa_ref.shape[0]
    qs = _stack_heads(qa_ref[...], masks)
    s_c = _dot_nt(qs, ck_ref[...].astype(BF16)) + bias_c_ref[...]
    s_n = _dot_nt(qs, kn_ref[...]) + bias_n_ref[...]
    acc, l = _softmax_pv([(s_c, cv_ref[...].astype(BF16)), (s_n, vn_ref[...])])
    oa_ref[...] = _unstack_heads(acc / l, masks, t).astype(BF16)
    o = _memory_attention_tile(qm_ref[...], mk_ref[...].astype(BF16), mv_ref[...].astype(BF16), masks)
    om_ref[...] = o.astype(BF16)


def _sample_bias(table, t, c):
    i = jnp.arange(t)[:, None]
    rel_c = c + i - jnp.arange(c)[None, :]
    rel_n = i - jnp.arange(NEW_PAD)[None, :]
    look = lambda rel: table[:, jnp.clip(rel, -REL_CLIP, REL_CLIP) + REL_CLIP].astype(F32)
    bias_c = look(rel_c)
    bias_n = jnp.where((jnp.arange(NEW_PAD) < t)[None, None, :], look(rel_n), NEG_INF)
    return bias_c.reshape(H_A * t, c), bias_n.reshape(H_A * t, NEW_PAD)


def _pad_new(a, nb, t):
    w = a.shape[1]
    return jnp.pad(a.reshape(nb, t, w), ((0, 0), (0, NEW_PAD - t), (0, 0))).reshape(nb * NEW_PAD, w)


def _attn_am_sample(qa, kn_pad, vn_pad, ck, cv, qm, cmk, cmv, bias_c, bias_n, nb, t):
    c = ck.shape[1]
    n_mem = cmk.shape[1]
    row = lambda w: pl.BlockSpec((t, w), lambda b: (b, 0))
    return pl.pallas_call(
        _attn_am_sample_kernel,
        grid=(nb,),
        in_specs=[row(GROUP_W),
                  pl.BlockSpec((NEW_PAD, GROUP_W), lambda b: (b, 0)),
                  pl.BlockSpec((NEW_PAD, GROUP_W), lambda b: (b, 0)),
                  pl.BlockSpec((None, c, GROUP_W), lambda b: (b, 0, 0)),
                  pl.BlockSpec((None, c, GROUP_W), lambda b: (b, 0, 0)),
                  row(GROUP_W),
                  pl.BlockSpec((None, n_mem, GROUP_W), lambda b: (b, 0, 0)),
                  pl.BlockSpec((None, n_mem, GROUP_W), lambda b: (b, 0, 0)),
                  pl.BlockSpec(bias_c.shape, lambda b: (0, 0)),
                  pl.BlockSpec(bias_n.shape, lambda b: (0, 0))],
        out_specs=[row(GROUP_W), row(GROUP_W)],
        out_shape=[jax.ShapeDtypeStruct((nb * t, GROUP_W), BF16)] * 2,
        compiler_params=pltpu.CompilerParams(
            dimension_semantics=("arbitrary",), vmem_limit_bytes=VMEM_LIMIT),
        name="attn_am_sample",
    )(qa, kn_pad, vn_pad, ck, cv, qm, cmk, cmv, bias_c, bias_n)


def _sb_sample_kernel(q_ref, kn_ref, vn_ref, ck_ref, cv_ref, t_ref, o_ref, acc_ref, carry_ref):
    masks = _head_masks()
    t = q_ref.shape[0]
    tmat = t_ref[...]
    qs = _stack_heads(q_ref[...], masks)
    rows = qs.shape[0]
    past = ck_ref.shape[0]

    q_idx = lax.broadcasted_iota(jnp.int32, (rows, 1), 0) & (t - 1)
    k_idx = lax.broadcasted_iota(jnp.int32, (1, NEW_PAD), 1)
    pv, carry = _sb_block(qs, kn_ref[...], vn_ref[...], t_ref[0:NEW_PAD, 0:NEW_PAD],
                          jnp.zeros((rows, 1), F32), k_idx < q_idx)
    acc_ref[...] = pv
    carry_ref[...] = carry

    nblk = past // SB_KB

    def body(it, _):
        kk = pl.multiple_of((nblk - 1 - it) * SB_KB, SB_KB)
        pv, carry = _sb_block(qs, ck_ref[pl.ds(kk, SB_KB), :].astype(BF16),
                              cv_ref[pl.ds(kk, SB_KB), :].astype(BF16), tmat, carry_ref[...], None)
        acc_ref[...] += pv
        carry_ref[...] = carry
        return 0

    lax.fori_loop(0, nblk, body, 0)
    o_ref[...] = _unstack_heads(acc_ref[...], masks, t).astype(BF16)


def _sb_sample(qb, kn_pad, vn_pad, ck, cv, nb, t):
    past = ck.shape[1]
    assert past % SB_KB == 0 and (t & (t - 1)) == 0 and t <= NEW_PAD
    groups = W_B // GROUP_W
    rows = HEADS_PER_GROUP * t
    return pl.pallas_call(
        _sb_sample_kernel,
        grid=(nb, groups),
        in_specs=[pl.BlockSpec((t, GROUP_W), lambda b, g: (b, g)),
                  pl.BlockSpec((NEW_PAD, GROUP_W), lambda b, g: (b, g)),
                  pl.BlockSpec((NEW_PAD, GROUP_W), lambda b, g: (b, g)),
                  pl.BlockSpec((None, past, GROUP_W), lambda b, g: (b, 0, g)),
                  pl.BlockSpec((None, past, GROUP_W), lambda b, g: (b, 0, g)),
                  pl.BlockSpec((SB_KB, SB_KB), lambda b, g: (0, 0))],
        out_specs=pl.BlockSpec((t, GROUP_W), lambda b, g: (b, g)),
        out_shape=jax.ShapeDtypeStruct((nb * t, W_B), BF16),
        scratch_shapes=[pltpu.VMEM((rows, GROUP_W), F32), pltpu.VMEM((rows, 1), F32)],
        compiler_params=pltpu.CompilerParams(
            dimension_semantics=("arbitrary", "arbitrary"), vmem_limit_bytes=VMEM_LIMIT),
        name="stick_breaking_sample",
    )(qb, kn_pad, vn_pad, ck, cv, _cumsum_matrix(SB_KB))


def _merge_kernel(alpha, x_ref, oa_ref, ob_ref, om_ref, wg_ref, bg_ref, wpa_ref, wpb_ref, wpm_ref,
                  wo_ref, g_ref, b_ref, h_ref):
    x = x_ref[...]
    xb = x.astype(BF16)
    d = x.shape[1]

    def gate(k):
        return _sigmoid(_dot(xb, wg_ref[:, k * d:(k + 1) * d]) + bg_ref[:, k * d:(k + 1) * d])

    hm = gate(0) * _dot(oa_ref[...], wpa_ref[...])
    hm = hm + gate(1) * _dot(ob_ref[...], wpb_ref[...])
    hm = hm + gate(2) * _dot(om_ref[...], wpm_ref[...])
    mix = _dot(hm.astype(BF16), wo_ref[...])
    h_ref[...] = _layer_norm(alpha * x + mix, g_ref[...], b_ref[...])


def _merge(x2d, oa, ob, om, wg, bg, wpa, wpb, wpm, wo, g1, b1, alpha):
    rows, d = x2d.shape
    tm = min(ROW_TILE, rows)
    assert rows % tm == 0
    row = lambda w: pl.BlockSpec((tm, w), lambda i: (i, 0))
    full = lambda a: pl.BlockSpec(a.shape, lambda i: (0, 0))
    return pl.pallas_call(
        functools.partial(_merge_kernel, alpha),
        grid=(rows // tm,),
        in_specs=[row(d), row(W_A), row(W_B), row(W_M), full(wg), full(bg), full(wpa), full(wpb),
                  full(wpm), full(wo), full(g1), full(b1)],
        out_specs=row(d),
        out_shape=jax.ShapeDtypeStruct((rows, d), F32),
        compiler_params=pltpu.CompilerParams(
            dimension_semantics=("arbitrary",), vmem_limit_bytes=VMEM_LIMIT),
        name="merge_ln1",
    )(x2d, oa, ob, om, wg, bg, wpa, wpb, wpm, wo, g1, b1)


def _causal_conv3(u, hist, w, b):
    t = u.shape[0]
    r = lax.broadcasted_iota(jnp.int32, (HIST_ROWS, 1), 0)
    u1 = pltpu.roll(u, 1, 0)
    u2 = pltpu.roll(u, 2, 0)
    h1 = hist[HIST_ROWS - 1:HIST_ROWS, :]
    h2 = hist[HIST_ROWS - 2:HIST_ROWS - 1, :]
    head1 = jnp.where(r == 0, h1, u1[0:HIST_ROWS])
    head2 = jnp.where(r == 0, h2, jnp.where(r == 1, h1, u2[0:HIST_ROWS]))
    if t > HIST_ROWS:
        u1 = jnp.concatenate([head1, u1[HIST_ROWS:]], axis=0)
        u2 = jnp.concatenate([head2, u2[HIST_ROWS:]], axis=0)
    else:
        u1, u2 = head1, head2
    return w[0:1, :] * u2 + w[1:2, :] * u1 + w[2:3, :] * u + b


def _ffn_prompt_kernel(alpha, d_ff, h_ref, wup_ref, cw_ref, cb_ref, wdn_ref, g_ref, b_ref,
                       y_ref, tail_ref, hist_ref):
    t = pl.program_id(1)

    @pl.when(t == 0)
    def _():
        hist_ref[...] = jnp.zeros_like(hist_ref)

    h = h_ref[...]
    hb = h.astype(BF16)
    tm = h.shape[0]
    acc = jnp.zeros(h.shape, F32)
    for c in range(d_ff // FF_CHUNK):
        gs = slice(c * FF_CHUNK, (c + 1) * FF_CHUNK)
        vs = slice(d_ff + c * FF_CHUNK, d_ff + (c + 1) * FF_CHUNK)
        ug = _dot(hb, wup_ref[:, gs])
        uv = _dot(hb, wup_ref[:, vs])
        cg = _causal_conv3(ug, hist_ref[:, gs], cw_ref[:, gs], cb_ref[:, gs])
        cv = _causal_conv3(uv, hist_ref[:, vs], cw_ref[:, vs], cb_ref[:, vs])
        a = cg * _sigmoid(cg) * cv
        acc = acc + _dot(a.astype(BF16), wdn_ref[gs, :])
        hist_ref[:, gs] = ug[tm - HIST_ROWS:tm]
        hist_ref[:, vs] = uv[tm - HIST_ROWS:tm]
    y_ref[...] = _layer_norm(alpha * h + acc, g_ref[...], b_ref[...])
    tail_ref[...] = hist_ref[...]


def _ffn_prompt(h2d, wup, cw, cb, wdn, g2, b2, alpha, batch, seq):
    d = h2d.shape[1]
    d_ff = wdn.shape[0]
    assert d_ff % FF_CHUNK == 0
    tm = min(ROW_TILE, seq)
    assert seq % tm == 0
    nt = seq // tm
    full = lambda a: pl.BlockSpec(a.shape, lambda b, t: (0, 0))
    return pl.pallas_call(
        functools.partial(_ffn_prompt_kernel, alpha, d_ff),
        grid=(batch, nt),
        in_specs=[pl.BlockSpec((tm, d), lambda b, t: (b * nt + t, 0)),
                  full(wup), full(cw), full(cb), full(wdn), full(g2), full(b2)],
        out_specs=[pl.BlockSpec((tm, d), lambda b, t: (b * nt + t, 0)),
                   pl.BlockSpec((None, HIST_ROWS, 2 * d_ff), lambda b, t: (b, 0, 0))],
        out_shape=[jax.ShapeDtypeStruct((batch * seq, d), F32),
                   jax.ShapeDtypeStruct((batch, HIST_ROWS, 2 * d_ff), F32)],
        scratch_shapes=[pltpu.VMEM((HIST_ROWS, 2 * d_ff), F32)],
        compiler_params=pltpu.CompilerParams(
            dimension_semantics=("arbitrary", "arbitrary"), vmem_limit_bytes=VMEM_LIMIT),
        name="conv_ffn_prompt",
    )(h2d, wup, cw, cb, wdn, g2, b2)


def _ffn_sample_kernel(alpha, t, h_ref, sg_ref, sv_ref, wg_ref, wv_ref, cwg_ref, cwv_ref, cbg_ref,
                       cbv_ref, wdn_ref, g_ref, b_ref, y_ref, ug_ref, uv_ref, a_ref, acc_ref):
    c = pl.program_id(0)
    h = h_ref[...]
    hb = h.astype(BF16)
    nb = h.shape[0] // t
    ug_ref[...] = _dot(hb, wg_ref[...])
    uv_ref[...] = _dot(hb, wv_ref[...])

    def body(b, _):
        rows = pl.ds(pl.multiple_of(b * t, t), t)
        cg = _causal_conv3(ug_ref[rows, :], sg_ref[b], cwg_ref[...], cbg_ref[...])
        cv = _causal_conv3(uv_ref[rows, :], sv_ref[b], cwv_ref[...], cbv_ref[...])
        a_ref[rows, :] = (cg * _sigmoid(cg) * cv).astype(BF16)
        return 0

    lax.fori_loop(0, nb, body, 0)
    part = _dot(a_ref[...], wdn_ref[...])

    @pl.when(c == 0)
    def _():
        acc_ref[...] = part

    @pl.when(c > 0)
    def _():
        acc_ref[...] += part

    @pl.when(c == pl.num_programs(0) - 1)
    def _():
        y_ref[...] = _layer_norm(alpha * h + acc_ref[...], g_ref[...], b_ref[...])


def _ffn_sample(h2d, state_pad, wup, cw, cb, wdn, g2, b2, alpha, t):
    rows, d = h2d.shape
    d_ff = wdn.shape[0]
    nb = rows // t
    assert d_ff % FF_CHUNK == 0 and t % 16 == 0
    nc = d_ff // FF_CHUNK
    full2 = lambda a: pl.BlockSpec(a.shape, lambda c: (0, 0))
    gate_cols = lambda r: pl.BlockSpec((r, FF_CHUNK), lambda c: (0, c))
    val_cols = lambda r: pl.BlockSpec((r, FF_CHUNK), lambda c: (0, nc + c))
    return pl.pallas_call(
        functools.partial(_ffn_sample_kernel, alpha, t),
        grid=(nc,),
        in_specs=[full2(h2d),
                  pl.BlockSpec((nb, HIST_ROWS, FF_CHUNK), lambda c: (0, 0, c)),
                  pl.BlockSpec((nb, HIST_ROWS, FF_CHUNK), lambda c: (0, 0, nc + c)),
                  gate_cols(d), val_cols(d), gate_cols(CONV_W), val_cols(CONV_W),
                  gate_cols(1), val_cols(1),
                  pl.BlockSpec((FF_CHUNK, d), lambda c: (c, 0)), full2(g2), full2(b2)],
        out_specs=[pl.BlockSpec((rows, d), lambda c: (0, 0)),
                   pl.BlockSpec((rows, FF_CHUNK), lambda c: (0, c)),
                   pl.BlockSpec((rows, FF_CHUNK), lambda c: (0, c))],
        out_shape=[jax.ShapeDtypeStruct((rows, d), F32),
                   jax.ShapeDtypeStruct((rows, d_ff), F32),
                   jax.ShapeDtypeStruct((rows, d_ff), F32)],
        scratch_shapes=[pltpu.VMEM((rows, FF_CHUNK), BF16), pltpu.VMEM((rows, d), F32)],
        compiler_params=pltpu.CompilerParams(
            dimension_semantics=("arbitrary",), vmem_limit_bytes=VMEM_LIMIT),
        name="conv_ffn_sample",
    )(h2d, state_pad, state_pad, wup, wup, cw, cw, cb, cb, wdn, g2, b2)


def kernel(x_prompt, x_sample, cache_a_k, cache_a_v, cache_b_k, cache_b_v, cache_mem_k, cache_mem_v, state_ffn_conv, mem_prompt, w_in, rel_bias, w_mem_kv, w_pa, w_pb, w_pm, w_gate, b_gate, w_o, ln1_g, ln1_b, w_up, conv_w, conv_b, w_down, ln2_g, ln2_b):
    depth = w_in.shape[0]
    batch, seq, d = x_prompt.shape
    nb, t, _ = x_sample.shape
    n_mem = mem_prompt.shape[1]
    past = cache_b_k.shape[2]
    a_cache = cache_a_k.shape[2]
    a_keep = min(BAND, seq)
    d_ff = w_down.shape[1]
    alpha = (2 * depth) ** 0.25
    assert conv_w.shape[1] == CONV_W

    yp = x_prompt.reshape(batch * seq, d)
    ys = x_sample.reshape(nb * t, d)
    outs = {k: [] for k in ("pak", "pav", "pbk", "pbv", "pmk", "pmv", "pconv",
                            "sak", "sav", "sbk", "sbv", "sconv")}
    for l in range(depth):
        bf = lambda a: a[l].astype(BF16)
        w_in_l, w_mem_l = bf(w_in), bf(w_mem_kv)
        wg, wpa, wpb, wpm, wo = bf(w_gate), bf(w_pa), bf(w_pb), bf(w_pm), bf(w_o)
        wup, wdn = bf(w_up), bf(w_down)
        row = lambda a: a[l][None, :]
        bg, g1, b1, g2, b2, cb = row(b_gate), row(ln1_g), row(ln1_b), row(ln2_g), row(ln2_b), row(conv_b)
        cw = conv_w[l]

        qa, ka, va, qb, kb, vb, qm, ka32, va32, kb32, vb32 = _in_projection(yp, w_in_l)
        mk32, mv32, mk, mv = _memory_kv(mem_prompt.reshape(batch * n_mem, d), w_mem_l)
        oa, om = _attn_am_prompt(qa, ka, va, qm, mk, mv, _pair_bias(rel_bias[l]), batch, seq, n_mem)
        ob = _sb_prompt(qb, kb, vb, batch, seq)
        h = _merge(yp, oa, ob, om, wg, bg, wpa, wpb, wpm, wo, g1, b1, alpha)
        yp, tail = _ffn_prompt(h, wup, cw, cb, wdn, g2, b2, alpha, batch, seq)
        outs["pak"].append(ka32.reshape(batch, seq, H_A, HEAD_DIM)[:, -a_keep:])
        outs["pav"].append(va32.reshape(batch, seq, H_A, HEAD_DIM)[:, -a_keep:])
        outs["pbk"].append(kb32.reshape(batch, seq, H_B, HEAD_DIM))
        outs["pbv"].append(vb32.reshape(batch, seq, H_B, HEAD_DIM))
        outs["pmk"].append(mk32.reshape(batch, n_mem, H_M, HEAD_DIM))
        outs["pmv"].append(mv32.reshape(batch, n_mem, H_M, HEAD_DIM))
        outs["pconv"].append(tail[:, HIST_ROWS - (CONV_W - 1):])

        qa, ka, va, qb, kb, vb, qm, ka32, va32, kb32, vb32 = _in_projection(ys, w_in_l)
        bias_c, bias_n = _sample_bias(rel_bias[l], t, a_cache)
        oa, om = _attn_am_sample(
            qa, _pad_new(ka, nb, t), _pad_new(va, nb, t),
            cache_a_k[l].reshape(nb, a_cache, W_A), cache_a_v[l].reshape(nb, a_cache, W_A), qm,
            cache_mem_k[l].reshape(nb, n_mem, W_M), cache_mem_v[l].reshape(nb, n_mem, W_M),
            bias_c, bias_n, nb, t)
        ob = _sb_sample(qb, _pad_new(kb, nb, t), _pad_new(vb, nb, t),
                        cache_b_k[l].reshape(nb, past, W_B), cache_b_v[l].reshape(nb, past, W_B), nb, t)
        h = _merge(ys, oa, ob, om, wg, bg, wpa, wpb, wpm, wo, g1, b1, alpha)
        state_pad = jnp.pad(state_ffn_conv[l], ((0, 0), (HIST_ROWS - (CONV_W - 1), 0), (0, 0)))
        assert t >= CONV_W - 1
        ys, ug, uv = _ffn_sample(h, state_pad, wup, cw, cb, wdn, g2, b2, alpha, t)
        tail_rows = lambda u: u.reshape(nb, t, d_ff)[:, -(CONV_W - 1):]
        outs["sak"].append(ka32.reshape(nb, t, H_A, HEAD_DIM))
        outs["sav"].append(va32.reshape(nb, t, H_A, HEAD_DIM))
        outs["sbk"].append(kb32.reshape(nb, t, H_B, HEAD_DIM))
        outs["sbv"].append(vb32.reshape(nb, t, H_B, HEAD_DIM))
        outs["sconv"].append(jnp.concatenate([tail_rows(ug), tail_rows(uv)], axis=-1))

    st = lambda k: jnp.stack(outs[k])
    return (yp.reshape(batch, seq, d), ys.reshape(nb, t, d),
            st("pak"), st("pav"), st("pbk"), st("pbv"), st("pmk"), st("pmv"), st("pconv"),
            st("sak"), st("sav"), st("sbk"), st("sbv"), st("sconv"))
```

```python
import functools
from typing import NamedTuple

import jax
import jax.numpy as jnp
from jax import lax
from jax.experimental import pallas as pl
from jax.experimental.pallas import tpu as pltpu

F32 = jnp.float32
BF16 = jnp.bfloat16

CHUNK = 64
LEFT_CHUNKS = 8
HEAD_DIM = 64
H_A = 4
H_B = 8
H_M = 4
REL_CLIP = 256
CONV_W = 3
LN_EPS = 1e-5
NEG_INF = -1e30
W_A = H_A * HEAD_DIM
W_B = H_B * HEAD_DIM
W_M = H_M * HEAD_DIM
BAND = LEFT_CHUNKS * CHUNK
HEADS_PER_GROUP = 4
GROUP_W = HEADS_PER_GROUP * HEAD_DIM
Q_SCALE = HEAD_DIM ** -0.5

ROW_TILE = 512
PAIR = 2 * CHUNK
PAIR_WIN = BAND + PAIR
SB_TQ = 256
SB_KB = 256
LOG2_E = 1.4426950408889634
MASKED_SCORE = -1e30
SB_CHAINS = 2
FF_CHUNK = 256
HIST_ROWS = 8
VMEM_LIMIT = 56 * 1024 * 1024


def _dot(a, b):
    return jnp.dot(a, b, preferred_element_type=F32)


def _dot_nt(a, b):
    return lax.dot_general(a, b, (((1,), (1,)), ((), ())), preferred_element_type=F32)


def _head_masks():
    lane = lax.broadcasted_iota(jnp.int32, (1, GROUP_W), 1)
    return [(lane >> 6) == h for h in range(HEADS_PER_GROUP)]


def _stack_heads(q, masks):
    zero = jnp.zeros_like(q)
    return jnp.concatenate([jnp.where(m, q, zero) for m in masks], axis=0)


def _unstack_heads(pv, masks, t):
    out = jnp.where(masks[0], pv[0:t], 0.0)
    for h in range(1, HEADS_PER_GROUP):
        out = out + jnp.where(masks[h], pv[h * t:(h + 1) * t], 0.0)
    return out


def _layer_norm(x, g, b):
    mu = jnp.mean(x, axis=-1, keepdims=True)
    xc = x - mu
    var = jnp.mean(xc * xc, axis=-1, keepdims=True)
    return xc * lax.rsqrt(var + LN_EPS) * g + b


def _sigmoid(x):
    return 1.0 / (1.0 + jnp.exp(-x))


def _inproj_kernel(x_ref, w_ref, qa, ka, va, qb, kb, vb, qm, ka32, va32, kb32, vb32):
    x = x_ref[...].astype(BF16)
    o = 0

    def seg(width):
        nonlocal o
        y = _dot(x, w_ref[:, o:o + width])
        o += width
        return y

    qa[...] = (seg(W_A) * Q_SCALE).astype(BF16)
    y = seg(W_A); ka[...] = y.astype(BF16); ka32[...] = y
    y = seg(W_A); va[...] = y.astype(BF16); va32[...] = y
    qb[...] = (seg(W_B) * (Q_SCALE * LOG2_E)).astype(BF16)
    y = seg(W_B); kb[...] = y.astype(BF16); kb32[...] = y
    y = seg(W_B); vb[...] = y.astype(BF16); vb32[...] = y
    qm[...] = (seg(W_M) * Q_SCALE).astype(BF16)


def _in_projection(x2d, w_in_bf):
    rows, d = x2d.shape
    tm = min(ROW_TILE, rows)
    assert rows % tm == 0
    d_in = w_in_bf.shape[1]
    widths = [W_A, W_A, W_A, W_B, W_B, W_B, W_M, W_A, W_A, W_B, W_B]
    dtypes = [BF16] * 7 + [F32] * 4
    return pl.pallas_call(
        _inproj_kernel,
        grid=(rows // tm,),
        in_specs=[pl.BlockSpec((tm, d), lambda i: (i, 0)),
                  pl.BlockSpec((d, d_in), lambda i: (0, 0))],
        out_specs=[pl.BlockSpec((tm, w), lambda i: (i, 0)) for w in widths],
        out_shape=[jax.ShapeDtypeStruct((rows, w), dt) for w, dt in zip(widths, dtypes)],
        compiler_params=pltpu.CompilerParams(
            dimension_semantics=("arbitrary",), vmem_limit_bytes=VMEM_LIMIT),
        name="in_projection",
    )(x2d, w_in_bf)


def _memkv_kernel(m_ref, w_ref, k32, v32, kbf, vbf):
    y = _dot(m_ref[...].astype(BF16), w_ref[...])
    k = y[:, :W_M]
    v = y[:, W_M:]
    k32[...] = k
    v32[...] = v
    kbf[...] = k.astype(BF16)
    vbf[...] = v.astype(BF16)


def _memory_kv(mem2d, w_bf):
    rows, d = mem2d.shape
    tm = min(ROW_TILE, rows)
    assert rows % tm == 0
    return pl.pallas_call(
        _memkv_kernel,
        grid=(rows // tm,),
        in_specs=[pl.BlockSpec((tm, d), lambda i: (i, 0)),
                  pl.BlockSpec((d, 2 * W_M), lambda i: (0, 0))],
        out_specs=[pl.BlockSpec((tm, W_M), lambda i: (i, 0))] * 4,
        out_shape=[jax.ShapeDtypeStruct((rows, W_M), F32)] * 2
        + [jax.ShapeDtypeStruct((rows, W_M), BF16)] * 2,
        compiler_params=pltpu.CompilerParams(
            dimension_semantics=("arbitrary",), vmem_limit_bytes=VMEM_LIMIT),
        name="memory_kv",
    )(mem2d, w_bf)


def _softmax_pv(score_value_pairs):
    m = None
    for s, _ in score_value_pairs:
        ms = jnp.max(s, axis=-1, keepdims=True)
        m = ms if m is None else jnp.maximum(m, ms)
    acc = None
    l = None
    for s, v in score_value_pairs:
        e = jnp.exp(s - m)
        ls = jnp.sum(e, axis=-1, keepdims=True)
        pv = _dot(e.astype(BF16), v)
        acc = pv if acc is None else acc + pv
        l = ls if l is None else l + ls
    return acc, l


def _memory_attention_tile(q, mk, mv, masks):
    t = q.shape[0]
    qs = _stack_heads(q, masks)
    s = _dot_nt(qs, mk)
    acc, l = _softmax_pv([(s, mv)])
    return _unstack_heads(acc / l, masks, t)


def _attn_am_prompt_kernel(qa_ref, kp_ref, kc_ref, vp_ref, vc_ref, qm_ref, mk_ref, mv_ref,
                           bias_ref, oa_ref, om_ref, kwin, vwin):
    t = pl.program_id(1)
    tq = qa_ref.shape[0]
    masks = _head_masks()
    kwin[0:BAND, :] = kp_ref[...]
    kwin[BAND:BAND + tq, :] = kc_ref[...]
    vwin[0:BAND, :] = vp_ref[...]
    vwin[BAND:BAND + tq, :] = vc_ref[...]
    col = lax.broadcasted_iota(jnp.int32, (1, PAIR_WIN), 1)
    for p in range(tq // PAIR):
        r0 = p * PAIR
        q = qa_ref[r0:r0 + PAIR, :]
        qs = _stack_heads(q, masks)
        s = _dot_nt(qs, kwin[r0:r0 + PAIR_WIN, :])
        s = s + bias_ref[...]
        before_start = jnp.logical_and(t == 0, col < BAND - r0)
        s = jnp.where(before_start, NEG_INF, s)
        acc, l = _softmax_pv([(s, vwin[r0:r0 + PAIR_WIN, :])])
        oa_ref[r0:r0 + PAIR, :] = _unstack_heads(acc / l, masks, PAIR).astype(BF16)
    mk = mk_ref[...]
    mv = mv_ref[...]
    for p in range(tq // PAIR):
        r0 = p * PAIR
        o = _memory_attention_tile(qm_ref[r0:r0 + PAIR, :], mk, mv, masks)
        om_ref[r0:r0 + PAIR, :] = o.astype(BF16)


def _pair_bias(table):
    i = jnp.arange(PAIR)[:, None]
    j = jnp.arange(PAIR_WIN)[None, :]
    rel = BAND + i - j
    idx = jnp.clip(rel, -REL_CLIP, REL_CLIP) + REL_CLIP
    qc = i // CHUNK
    kc = j // CHUNK
    visible = jnp.logical_and(kc >= qc, kc <= qc + LEFT_CHUNKS)
    bias = jnp.where(visible[None], table[:, idx].astype(F32), NEG_INF)
    return bias.reshape(H_A * PAIR, PAIR_WIN)


def _attn_am_prompt(qa, ka, va, qm, mk, mv, bias, batch, seq, n_mem):
    tq = BAND
    assert seq % tq == 0
    nt = seq // tq
    cur = lambda b, t: (b * nt + t, 0)
    prev = lambda b, t: (b * nt + jnp.maximum(t - 1, 0), 0)
    tile = lambda im: pl.BlockSpec((tq, GROUP_W), im)
    return pl.pallas_call(
        _attn_am_prompt_kernel,
        grid=(batch, nt),
        in_specs=[tile(cur), tile(prev), tile(cur), tile(prev), tile(cur), tile(cur),
                  pl.BlockSpec((n_mem, GROUP_W), lambda b, t: (b, 0)),
                  pl.BlockSpec((n_mem, GROUP_W), lambda b, t: (b, 0)),
                  pl.BlockSpec(bias.shape, lambda b, t: (0, 0))],
        out_specs=[tile(cur), tile(cur)],
        out_shape=[jax.ShapeDtypeStruct((batch * seq, GROUP_W), BF16)] * 2,
        scratch_shapes=[pltpu.VMEM((BAND + tq, GROUP_W), BF16)] * 2,
        compiler_params=pltpu.CompilerParams(
            dimension_semantics=("arbitrary", "arbitrary"), vmem_limit_bytes=VMEM_LIMIT),
        name="attn_am_prompt",
    )(qa, ka, ka, va, va, qm, mk, mv, bias)


class _SbScratch(NamedTuple):
    qs: object
    z: object
    e: object
    c0: object
    acc: object
    carry: object


def _sb_scratch_shapes(rows):
    return [pltpu.VMEM((rows, GROUP_W), BF16), pltpu.VMEM((rows, SB_KB), F32),
            pltpu.VMEM((rows, SB_KB), F32), pltpu.VMEM((rows, 1), F32),
            pltpu.VMEM((rows, GROUP_W), F32), pltpu.VMEM((rows, 1), F32)]


def _sb_reset(s):
    s.e[...] = jnp.full(s.e.shape, MASKED_SCORE, F32)
    s.c0[...] = jnp.zeros_like(s.c0)
    s.acc[...] = jnp.zeros_like(s.acc)
    s.carry[...] = jnp.zeros_like(s.carry)


def _sb_scores(rows, s, k_blk, mask):
    z = _dot_nt(s.qs[rows, :], k_blk)
    if mask is not None:
        z = jnp.where(mask, z, MASKED_SCORE)
    s.z[rows, :] = z


def _sb_cumsum(rows, s, tmat):
    z = s.z[rows, :]
    sign = jnp.uint32(0x80000000)
    neg_abs = lax.bitcast_convert_type(lax.bitcast_convert_type(z, jnp.uint32) | sign, F32)
    sp = jnp.maximum(z, 0.0) + jnp.log(1.0 + jnp.exp2(neg_abs)) * LOG2_E
    hi = sp.astype(BF16)
    lo = (sp - hi.astype(F32)).astype(BF16)
    incl = _dot(hi, tmat) + _dot(lo, tmat)
    s.e[rows, :] = z - incl
    s.c0[rows, :] = incl[:, 0:1]


def _sb_apply(rows, s, v_blk):
    w = jnp.exp2(s.e[rows, :] - s.carry[rows, :])
    s.acc[rows, :] += _dot(w.astype(BF16), v_blk)
    s.carry[rows, :] += s.c0[rows, :]


def _row_chunks(n, parts):
    step = n // parts
    return [slice(c * step, (c + 1) * step) for c in range(parts)]


def _sb_prompt_kernel(q_ref, k_ref, v_ref, t_ref, o_ref, *scratch):
    i = pl.program_id(2)
    s = _SbScratch(*scratch)
    masks = _head_masks()
    tmat = t_ref[...]
    s.qs[...] = _stack_heads(q_ref[...], masks)
    _sb_reset(s)
    chunks = _row_chunks(s.qs.shape[0], SB_CHAINS)

    def key_block(ref, j):
        return ref[pl.ds(pl.multiple_of(j * SB_KB, SB_KB), SB_KB), :]

    k_pos = lax.broadcasted_iota(jnp.int32, (1, SB_KB), 1)
    for rows in chunks:
        n = rows.stop - rows.start
        q_pos = (rows.start + lax.broadcasted_iota(jnp.int32, (n, 1), 0)) & (SB_TQ - 1)
        _sb_scores(rows, s, key_block(k_ref, i), k_pos < q_pos)

    def body(t, _):
        v_blk = key_block(v_ref, jnp.minimum(i - t + 2, i))
        k_blk = key_block(k_ref, i - t)
        for rows in chunks:
            _sb_apply(rows, s, v_blk)
            _sb_cumsum(rows, s, tmat)
            _sb_scores(rows, s, k_blk, None)
        return 0

    lax.fori_loop(1, i + 1, body, 0)
    v_blk = key_block(v_ref, jnp.minimum(1, i))
    for rows in chunks:
        _sb_apply(rows, s, v_blk)
        _sb_cumsum(rows, s, tmat)
    v_blk = key_block(v_ref, 0)
    for rows in chunks:
        _sb_apply(rows, s, v_blk)
    o_ref[...] = _unstack_heads(s.acc[...], masks, SB_TQ).astype(BF16)


def _cumsum_matrix(n):
    j = jnp.arange(n)[:, None]
    s = jnp.arange(n)[None, :]
    return (j >= s).astype(BF16)


def _sb_prompt(qb, kb, vb, batch, seq):
    assert seq % SB_KB == 0 and SB_KB == SB_TQ
    nq = seq // SB_TQ
    groups = W_B // GROUP_W
    rows = HEADS_PER_GROUP * SB_TQ
    return pl.pallas_call(
        _sb_prompt_kernel,
        grid=(batch, groups, nq),
        in_specs=[pl.BlockSpec((SB_TQ, GROUP_W), lambda b, g, i: (b * nq + i, g)),
                  pl.BlockSpec((seq, GROUP_W), lambda b, g, i: (b, g)),
                  pl.BlockSpec((seq, GROUP_W), lambda b, g, i: (b, g)),
                  pl.BlockSpec((SB_KB, SB_KB), lambda b, g, i: (0, 0))],
        out_specs=pl.BlockSpec((SB_TQ, GROUP_W), lambda b, g, i: (b * nq + i, g)),
        out_shape=jax.ShapeDtypeStruct((batch * seq, W_B), BF16),
        scratch_shapes=_sb_scratch_shapes(rows),
        compiler_params=pltpu.CompilerParams(
            dimension_semantics=("arbitrary", "arbitrary", "arbitrary"), vmem_limit_bytes=VMEM_LIMIT),
        name="stick_breaking_prompt",
    )(qb, kb, vb, _cumsum_matrix(SB_KB))


NEW_PAD = SB_KB


def _attn_am_sample_kernel(qa_ref, kn_ref, vn_ref, ck_ref, cv_ref, qm_ref, mk_ref, mv_ref,
                           bias_c_ref, bias_n_ref, oa_ref, om_ref):
    masks = _head_masks()
    t = qa_ref.shape[0]
    qs = _stack_heads(qa_ref[...], masks)
    s_c = _dot_nt(qs, ck_ref[...].astype(BF16)) + bias_c_ref[...]
    s_n = _dot_nt(qs, kn_ref[...]) + bias_n_ref[...]
    acc, l = _softmax_pv([(s_c, cv_ref[...].astype(BF16)), (s_n, vn_ref[...])])
    oa_ref[...] = _unstack_heads(acc / l, masks, t).astype(BF16)
    o = _memory_attention_tile(qm_ref[...], mk_ref[...].astype(BF16), mv_ref[...].astype(BF16), masks)
    om_ref[...] = o.astype(BF16)


def _sample_bias(table, t, c):
    i = jnp.arange(t)[:, None]
    rel_c = c + i - jnp.arange(c)[None, :]
    rel_n = i - jnp.arange(NEW_PAD)[None, :]
    look = lambda rel: table[:, jnp.clip(rel, -REL_CLIP, REL_CLIP) + REL_CLIP].astype(F32)
    bias_c = look(rel_c)
    bias_n = jnp.where((jnp.arange(NEW_PAD) < t)[None, None, :], look(rel_n), NEG_INF)
    return bias_c.reshape(H_A * t, c), bias_n.reshape(H_A * t, NEW_PAD)


def _pad_new(a, nb, t):
    w = a.shape[1]
    return jnp.pad(a.reshape(nb, t, w), ((0, 0), (0, NEW_PAD - t), (0, 0))).reshape(nb * NEW_PAD, w)


def _attn_am_sample(qa, kn_pad, vn_pad, ck, cv, qm, cmk, cmv, bias_c, bias_n, nb, t):
    c = ck.shape[1]
    n_mem = cmk.shape[1]
    row = lambda w: pl.BlockSpec((t, w), lambda b: (b, 0))
    return pl.pallas_call(
        _attn_am_sample_kernel,
        grid=(nb,),
        in_specs=[row(GROUP_W),
                  pl.BlockSpec((NEW_PAD, GROUP_W), lambda b: (b, 0)),
                  pl.BlockSpec((NEW_PAD, GROUP_W), lambda b: (b, 0)),
                  pl.BlockSpec((None, c, GROUP_W), lambda b: (b, 0, 0)),
                  pl.BlockSpec((None, c, GROUP_W), lambda b: (b, 0, 0)),
                  row(GROUP_W),
                  pl.BlockSpec((None, n_mem, GROUP_W), lambda b: (b, 0, 0)),
                  pl.BlockSpec((None, n_mem, GROUP_W), lambda b: (b, 0, 0)),
                  pl.BlockSpec(bias_c.shape, lambda b: (0, 0)),
                  pl.BlockSpec(bias_n.shape, lambda b: (0, 0))],
        out_specs=[row(GROUP_W), row(GROUP_W)],
        out_shape=[jax.ShapeDtypeStruct((nb * t, GROUP_W), BF16)] * 2,
        compiler_params=pltpu.CompilerParams(
            dimension_semantics=("arbitrary",), vmem_limit_bytes=VMEM_LIMIT),
        name="attn_am_sample",
    )(qa, kn_pad, vn_pad, ck, cv, qm, cmk, cmv, bias_c, bias_n)


def _sb_sample_kernel(q_ref, kn_ref, vn_ref, ck_ref, cv_ref, t_ref, o_ref, *scratch):
    s = _SbScratch(*scratch)
    masks = _head_masks()
    t = q_ref.shape[0]
    tmat = t_ref[...]
    s.qs[...] = _stack_heads(q_ref[...], masks)
    _sb_reset(s)
    n = s.qs.shape[0]
    rows = slice(0, n)
    nblk = ck_ref.shape[0] // SB_KB

    def cache_block(ref, j):
        return ref[pl.ds(pl.multiple_of(j * SB_KB, SB_KB), SB_KB), :].astype(BF16)

    q_idx = lax.broadcasted_iota(jnp.int32, (n, 1), 0) & (t - 1)
    k_idx = lax.broadcasted_iota(jnp.int32, (1, NEW_PAD), 1)
    _sb_scores(rows, s, kn_ref[...], k_idx < q_idx)
    _sb_cumsum(rows, s, tmat)
    _sb_apply(rows, s, vn_ref[...])

    def body(it, _):
        j = nblk - 1 - it
        _sb_scores(rows, s, cache_block(ck_ref, j), None)
        _sb_cumsum(rows, s, tmat)
        _sb_apply(rows, s, cache_block(cv_ref, j))
        return 0

    lax.fori_loop(0, nblk, body, 0)
    o_ref[...] = _unstack_heads(s.acc[...], masks, t).astype(BF16)


def _sb_sample(qb, kn_pad, vn_pad, ck, cv, nb, t):
    past = ck.shape[1]
    assert past % SB_KB == 0 and (t & (t - 1)) == 0 and t <= NEW_PAD
    groups = W_B // GROUP_W
    rows = HEADS_PER_GROUP * t
    return pl.pallas_call(
        _sb_sample_kernel,
        grid=(nb, groups),
        in_specs=[pl.BlockSpec((t, GROUP_W), lambda b, g: (b, g)),
                  pl.BlockSpec((NEW_PAD, GROUP_W), lambda b, g: (b, g)),
                  pl.BlockSpec((NEW_PAD, GROUP_W), lambda b, g: (b, g)),
                  pl.BlockSpec((None, past, GROUP_W), lambda b, g: (b, 0, g)),
                  pl.BlockSpec((None, past, GROUP_W), lambda b, g: (b, 0, g)),
                  pl.BlockSpec((SB_KB, SB_KB), lambda b, g: (0, 0))],
        out_specs=pl.BlockSpec((t, GROUP_W), lambda b, g: (b, g)),
        out_shape=jax.ShapeDtypeStruct((nb * t, W_B), BF16),
        scratch_shapes=_sb_scratch_shapes(rows),
        compiler_params=pltpu.CompilerParams(
            dimension_semantics=("arbitrary", "arbitrary"), vmem_limit_bytes=VMEM_LIMIT),
        name="stick_breaking_sample",
    )(qb, kn_pad, vn_pad, ck, cv, _cumsum_matrix(SB_KB))


def _merge_kernel(alpha, x_ref, oa_ref, ob_ref, om_ref, wg_ref, bg_ref, wpa_ref, wpb_ref, wpm_ref,
                  wo_ref, g_ref, b_ref, h_ref):
    x = x_ref[...]
    xb = x.astype(BF16)
    d = x.shape[1]

    def gate(k):
        return _sigmoid(_dot(xb, wg_ref[:, k * d:(k + 1) * d]) + bg_ref[:, k * d:(k + 1) * d])

    hm = gate(0) * _dot(oa_ref[...], wpa_ref[...])
    hm = hm + gate(1) * _dot(ob_ref[...], wpb_ref[...])
    hm = hm + gate(2) * _dot(om_ref[...], wpm_ref[...])
    mix = _dot(hm.astype(BF16), wo_ref[...])
    h_ref[...] = _layer_norm(alpha * x + mix, g_ref[...], b_ref[...])


def _merge(x2d, oa, ob, om, wg, bg, wpa, wpb, wpm, wo, g1, b1, alpha):
    rows, d = x2d.shape
    tm = min(ROW_TILE, rows)
    assert rows % tm == 0
    row = lambda w: pl.BlockSpec((tm, w), lambda i: (i, 0))
    full = lambda a: pl.BlockSpec(a.shape, lambda i: (0, 0))
    return pl.pallas_call(
        functools.partial(_merge_kernel, alpha),
        grid=(rows // tm,),
        in_specs=[row(d), row(W_A), row(W_B), row(W_M), full(wg), full(bg), full(wpa), full(wpb),
                  full(wpm), full(wo), full(g1), full(b1)],
        out_specs=row(d),
        out_shape=jax.ShapeDtypeStruct((rows, d), F32),
        compiler_params=pltpu.CompilerParams(
            dimension_semantics=("arbitrary",), vmem_limit_bytes=VMEM_LIMIT),
        name="merge_ln1",
    )(x2d, oa, ob, om, wg, bg, wpa, wpb, wpm, wo, g1, b1)


def _causal_conv3(u, hist, w, b):
    t = u.shape[0]
    r = lax.broadcasted_iota(jnp.int32, (HIST_ROWS, 1), 0)
    u1 = pltpu.roll(u, 1, 0)
    u2 = pltpu.roll(u, 2, 0)
    h1 = hist[HIST_ROWS - 1:HIST_ROWS, :]
    h2 = hist[HIST_ROWS - 2:HIST_ROWS - 1, :]
    head1 = jnp.where(r == 0, h1, u1[0:HIST_ROWS])
    head2 = jnp.where(r == 0, h2, jnp.where(r == 1, h1, u2[0:HIST_ROWS]))
    if t > HIST_ROWS:
        u1 = jnp.concatenate([head1, u1[HIST_ROWS:]], axis=0)
        u2 = jnp.concatenate([head2, u2[HIST_ROWS:]], axis=0)
    else:
        u1, u2 = head1, head2
    return w[0:1, :] * u2 + w[1:2, :] * u1 + w[2:3, :] * u + b


def _ffn_prompt_kernel(alpha, d_ff, h_ref, wup_ref, cw_ref, cb_ref, wdn_ref, g_ref, b_ref,
                       y_ref, tail_ref, hist_ref):
    t = pl.program_id(1)

    @pl.when(t == 0)
    def _():
        hist_ref[...] = jnp.zeros_like(hist_ref)

    h = h_ref[...]
    hb = h.astype(BF16)
    tm = h.shape[0]
    acc = jnp.zeros(h.shape, F32)
    for c in range(d_ff // FF_CHUNK):
        gs = slice(c * FF_CHUNK, (c + 1) * FF_CHUNK)
        vs = slice(d_ff + c * FF_CHUNK, d_ff + (c + 1) * FF_CHUNK)
        ug = _dot(hb, wup_ref[:, gs])
        uv = _dot(hb, wup_ref[:, vs])
        cg = _causal_conv3(ug, hist_ref[:, gs], cw_ref[:, gs], cb_ref[:, gs])
        cv = _causal_conv3(uv, hist_ref[:, vs], cw_ref[:, vs], cb_ref[:, vs])
        a = cg * _sigmoid(cg) * cv
        acc = acc + _dot(a.astype(BF16), wdn_ref[gs, :])
        hist_ref[:, gs] = ug[tm - HIST_ROWS:tm]
        hist_ref[:, vs] = uv[tm - HIST_ROWS:tm]
    y_ref[...] = _layer_norm(alpha * h + acc, g_ref[...], b_ref[...])
    tail_ref[...] = hist_ref[...]


def _ffn_prompt(h2d, wup, cw, cb, wdn, g2, b2, alpha, batch, seq):
    d = h2d.shape[1]
    d_ff = wdn.shape[0]
    assert d_ff % FF_CHUNK == 0
    tm = min(ROW_TILE, seq)
    assert seq % tm == 0
    nt = seq // tm
    full = lambda a: pl.BlockSpec(a.shape, lambda b, t: (0, 0))
    return pl.pallas_call(
        functools.partial(_ffn_prompt_kernel, alpha, d_ff),
        grid=(batch, nt),
        in_specs=[pl.BlockSpec((tm, d), lambda b, t: (b * nt + t, 0)),
                  full(wup), full(cw), full(cb), full(wdn), full(g2), full(b2)],
        out_specs=[pl.BlockSpec((tm, d), lambda b, t: (b * nt + t, 0)),
                   pl.BlockSpec((None, HIST_ROWS, 2 * d_ff), lambda b, t: (b, 0, 0))],
        out_shape=[jax.ShapeDtypeStruct((batch * seq, d), F32),
                   jax.ShapeDtypeStruct((batch, HIST_ROWS, 2 * d_ff), F32)],
        scratch_shapes=[pltpu.VMEM((HIST_ROWS, 2 * d_ff), F32)],
        compiler_params=pltpu.CompilerParams(
            dimension_semantics=("arbitrary", "arbitrary"), vmem_limit_bytes=VMEM_LIMIT),
        name="conv_ffn_prompt",
    )(h2d, wup, cw, cb, wdn, g2, b2)


def _ffn_sample_kernel(alpha, t, h_ref, sg_ref, sv_ref, wg_ref, wv_ref, cwg_ref, cwv_ref, cbg_ref,
                       cbv_ref, wdn_ref, g_ref, b_ref, y_ref, ug_ref, uv_ref, a_ref, acc_ref):
    c = pl.program_id(0)
    h = h_ref[...]
    hb = h.astype(BF16)
    nb = h.shape[0] // t
    ug_ref[...] = _dot(hb, wg_ref[...])
    uv_ref[...] = _dot(hb, wv_ref[...])

    def body(b, _):
        rows = pl.ds(pl.multiple_of(b * t, t), t)
        cg = _causal_conv3(ug_ref[rows, :], sg_ref[b], cwg_ref[...], cbg_ref[...])
        cv = _causal_conv3(uv_ref[rows, :], sv_ref[b], cwv_ref[...], cbv_ref[...])
        a_ref[rows, :] = (cg * _sigmoid(cg) * cv).astype(BF16)
        return 0

    lax.fori_loop(0, nb, body, 0)
    part = _dot(a_ref[...], wdn_ref[...])

    @pl.when(c == 0)
    def _():
        acc_ref[...] = part

    @pl.when(c > 0)
    def _():
        acc_ref[...] += part

    @pl.when(c == pl.num_programs(0) - 1)
    def _():
        y_ref[...] = _layer_norm(alpha * h + acc_ref[...], g_ref[...], b_ref[...])


def _ffn_sample(h2d, state_pad, wup, cw, cb, wdn, g2, b2, alpha, t):
    rows, d = h2d.shape
    d_ff = wdn.shape[0]
    nb = rows // t
    assert d_ff % FF_CHUNK == 0 and t % 16 == 0
    nc = d_ff // FF_CHUNK
    full2 = lambda a: pl.BlockSpec(a.shape, lambda c: (0, 0))
    gate_cols = lambda r: pl.BlockSpec((r, FF_CHUNK), lambda c: (0, c))
    val_cols = lambda r: pl.BlockSpec((r, FF_CHUNK), lambda c: (0, nc + c))
    return pl.pallas_call(
        functools.partial(_ffn_sample_kernel, alpha, t),
        grid=(nc,),
        in_specs=[full2(h2d),
                  pl.BlockSpec((nb, HIST_ROWS, FF_CHUNK), lambda c: (0, 0, c)),
                  pl.BlockSpec((nb, HIST_ROWS, FF_CHUNK), lambda c: (0, 0, nc + c)),
                  gate_cols(d), val_cols(d), gate_cols(CONV_W), val_cols(CONV_W),
                  gate_cols(1), val_cols(1),
                  pl.BlockSpec((FF_CHUNK, d), lambda c: (c, 0)), full2(g2), full2(b2)],
        out_specs=[pl.BlockSpec((rows, d), lambda c: (0, 0)),
                   pl.BlockSpec((rows, FF_CHUNK), lambda c: (0, c)),
                   pl.BlockSpec((rows, FF_CHUNK), lambda c: (0, c))],
        out_shape=[jax.ShapeDtypeStruct((rows, d), F32),
                   jax.ShapeDtypeStruct((rows, d_ff), F32),
                   jax.ShapeDtypeStruct((rows, d_ff), F32)],
        scratch_shapes=[pltpu.VMEM((rows, FF_CHUNK), BF16), pltpu.VMEM((rows, d), F32)],
        compiler_params=pltpu.CompilerParams(
            dimension_semantics=("arbitrary",), vmem_limit_bytes=VMEM_LIMIT),
        name="conv_ffn_sample",
    )(h2d, state_pad, state_pad, wup, wup, cw, cw, cb, cb, wdn, g2, b2)


def kernel(x_prompt, x_sample, cache_a_k, cache_a_v, cache_b_k, cache_b_v, cache_mem_k, cache_mem_v, state_ffn_conv, mem_prompt, w_in, rel_bias, w_mem_kv, w_pa, w_pb, w_pm, w_gate, b_gate, w_o, ln1_g, ln1_b, w_up, conv_w, conv_b, w_down, ln2_g, ln2_b):
    depth = w_in.shape[0]
    batch, seq, d = x_prompt.shape
    nb, t, _ = x_sample.shape
    n_mem = mem_prompt.shape[1]
    past = cache_b_k.shape[2]
    a_cache = cache_a_k.shape[2]
    a_keep = min(BAND, seq)
    d_ff = w_down.shape[1]
    alpha = (2 * depth) ** 0.25
    assert conv_w.shape[1] == CONV_W

    yp = x_prompt.reshape(batch * seq, d)
    ys = x_sample.reshape(nb * t, d)
    outs = {k: [] for k in ("pak", "pav", "pbk", "pbv", "pmk", "pmv", "pconv",
                            "sak", "sav", "sbk", "sbv", "sconv")}
    for l in range(depth):
        bf = lambda a: a[l].astype(BF16)
        w_in_l, w_mem_l = bf(w_in), bf(w_mem_kv)
        wg, wpa, wpb, wpm, wo = bf(w_gate), bf(w_pa), bf(w_pb), bf(w_pm), bf(w_o)
        wup, wdn = bf(w_up), bf(w_down)
        row = lambda a: a[l][None, :]
        bg, g1, b1, g2, b2, cb = row(b_gate), row(ln1_g), row(ln1_b), row(ln2_g), row(ln2_b), row(conv_b)
        cw = conv_w[l]

        qa, ka, va, qb, kb, vb, qm, ka32, va32, kb32, vb32 = _in_projection(yp, w_in_l)
        mk32, mv32, mk, mv = _memory_kv(mem_prompt.reshape(batch * n_mem, d), w_mem_l)
        oa, om = _attn_am_prompt(qa, ka, va, qm, mk, mv, _pair_bias(rel_bias[l]), batch, seq, n_mem)
        ob = _sb_prompt(qb, kb, vb, batch, seq)
        h = _merge(yp, oa, ob, om, wg, bg, wpa, wpb, wpm, wo, g1, b1, alpha)
        yp, tail = _ffn_prompt(h, wup, cw, cb, wdn, g2, b2, alpha, batch, seq)
        outs["pak"].append(ka32.reshape(batch, seq, H_A, HEAD_DIM)[:, -a_keep:])
        outs["pav"].append(va32.reshape(batch, seq, H_A, HEAD_DIM)[:, -a_keep:])
        outs["pbk"].append(kb32.reshape(batch, seq, H_B, HEAD_DIM))
        outs["pbv"].append(vb32.reshape(batch, seq, H_B, HEAD_DIM))
        outs["pmk"].append(mk32.reshape(batch, n_mem, H_M, HEAD_DIM))
        outs["pmv"].append(mv32.reshape(batch, n_mem, H_M, HEAD_DIM))
        outs["pconv"].append(tail[:, HIST_ROWS - (CONV_W - 1):])

        qa, ka, va, qb, kb, vb, qm, ka32, va32, kb32, vb32 = _in_projection(ys, w_in_l)
        bias_c, bias_n = _sample_bias(rel_bias[l], t, a_cache)
        oa, om = _attn_am_sample(
            qa, _pad_new(ka, nb, t), _pad_new(va, nb, t),
            cache_a_k[l].reshape(nb, a_cache, W_A), cache_a_v[l].reshape(nb, a_cache, W_A), qm,
            cache_mem_k[l].reshape(nb, n_mem, W_M), cache_mem_v[l].reshape(nb, n_mem, W_M),
            bias_c, bias_n, nb, t)
        ob = _sb_sample(qb, _pad_new(kb, nb, t), _pad_new(vb, nb, t),
                        cache_b_k[l].reshape(nb, past, W_B), cache_b_v[l].reshape(nb, past, W_B), nb, t)
        h = _merge(ys, oa, ob, om, wg, bg, wpa, wpb, wpm, wo, g1, b1, alpha)
        state_pad = jnp.pad(state_ffn_conv[l], ((0, 0), (HIST_ROWS - (CONV_W - 1), 0), (0, 0)))
        assert t >= CONV_W - 1
        ys, ug, uv = _ffn_sample(h, state_pad, wup, cw, cb, wdn, g2, b2, alpha, t)
        tail_rows = lambda u: u.reshape(nb, t, d_ff)[:, -(CONV_W - 1):]
        outs["sak"].append(ka32.reshape(nb, t, H_A, HEAD_DIM))
        outs["sav"].append(va32.reshape(nb, t, H_A, HEAD_DIM))
        outs["sbk"].append(kb32.reshape(nb, t, H_B, HEAD_DIM))
        outs["sbv"].append(vb32.reshape(nb, t, H_B, HEAD_DIM))
        outs["sconv"].append(jnp.concatenate([tail_rows(ug), tail_rows(uv)], axis=-1))

    st = lambda k: jnp.stack(outs[k])
    return (yp.reshape(batch, seq, d), ys.reshape(nb, t, d),
            st("pak"), st("pav"), st("pbk"), st("pbv"), st("pmk"), st("pmv"), st("pconv"),
            st("sak"), st("sav"), st("sbk"), st("sbv"), st("sconv"))
```

```python
import functools
from typing import NamedTuple

import jax
import jax.numpy as jnp
from jax import lax
from jax.experimental import pallas as pl
from jax.experimental.pallas import tpu as pltpu

F32 = jnp.float32
BF16 = jnp.bfloat16

CHUNK = 64
LEFT_CHUNKS = 8
HEAD_DIM = 64
H_A = 4
H_B = 8
H_M = 4
REL_CLIP = 256
CONV_W = 3
LN_EPS = 1e-5
NEG_INF = -1e30
W_A = H_A * HEAD_DIM
W_B = H_B * HEAD_DIM
W_M = H_M * HEAD_DIM
BAND = LEFT_CHUNKS * CHUNK
HEADS_PER_GROUP = 4
GROUP_W = HEADS_PER_GROUP * HEAD_DIM
Q_SCALE = HEAD_DIM ** -0.5

ROW_TILE = 512
PAIR = 2 * CHUNK
PAIR_WIN = BAND + PAIR
SB_TQ = 256
SB_KB = 256
SB_CHAINS = 2
LOG2_E = 1.4426950408889634
MASKED_SCORE = -1e30
SB_DONE = 160.0
NEW_PAD = SB_KB
FF_CHUNK = 256
HIST_ROWS = 8
VMEM_LIMIT = 56 * 1024 * 1024


def _dot(a, b):
    return jnp.dot(a, b, preferred_element_type=F32)


def _dot_nt(a, b):
    return lax.dot_general(a, b, (((1,), (1,)), ((), ())), preferred_element_type=F32)


def _head_masks():
    lane = lax.broadcasted_iota(jnp.int32, (1, GROUP_W), 1)
    return [(lane >> 6) == h for h in range(HEADS_PER_GROUP)]


def _stack_heads(q, masks):
    zero = jnp.zeros_like(q)
    return jnp.concatenate([jnp.where(m, q, zero) for m in masks], axis=0)


def _unstack_heads(pv, masks, t):
    out = jnp.where(masks[0], pv[0:t], 0.0)
    for h in range(1, HEADS_PER_GROUP):
        out = out + jnp.where(masks[h], pv[h * t:(h + 1) * t], 0.0)
    return out


def _layer_norm(x, g, b):
    mu = jnp.mean(x, axis=-1, keepdims=True)
    xc = x - mu
    var = jnp.mean(xc * xc, axis=-1, keepdims=True)
    return xc * lax.rsqrt(var + LN_EPS) * g + b


def _sigmoid(x):
    return 1.0 / (1.0 + jnp.exp(-x))


def _project(x_ref, w_ref):
    x = x_ref[...].astype(BF16)
    o = 0
    for width, scale in ((W_A, Q_SCALE), (W_A, None), (W_A, None),
                         (W_B, Q_SCALE * LOG2_E),
                         (W_B, None), (W_B, None), (W_M, Q_SCALE)):
        y = _dot(x, w_ref[:, o:o + width])
        o += width
        yield y if scale is None else y * scale


def _inproj_prompt_kernel(x_ref, w_ref, qa, ka, va, qb, kb, vb, qm, ka_t, va_t, kb_t, vb_t):
    last = pl.program_id(1) == pl.num_programs(1) - 1
    for y, out, out_t, tail_only in zip(_project(x_ref, w_ref), (qa, ka, va, qb, kb, vb, qm),
                                        (None, ka_t, va_t, None, kb_t, vb_t, None),
                                        (False, True, True, False, False, False, False)):
        out[...] = y.astype(BF16)
        if out_t is None:
            continue
        if tail_only:
            @pl.when(last)
            def _():
                out_t[...] = y.T
        else:
            out_t[...] = y.T


def _in_projection_prompt(x2d, w_in_bf, batch, seq):
    rows, d = x2d.shape
    tm = ROW_TILE
    assert seq % tm == 0
    nt = seq // tm
    d_in = w_in_bf.shape[1]
    widths = [W_A, W_A, W_A, W_B, W_B, W_B, W_M]
    row = lambda w: pl.BlockSpec((tm, w), lambda b, t: (b * nt + t, 0))
    return pl.pallas_call(
        _inproj_prompt_kernel,
        grid=(batch, nt),
        in_specs=[row(d), pl.BlockSpec((d, d_in), lambda b, t: (0, 0))],
        out_specs=[row(w) for w in widths]
        + [pl.BlockSpec((None, W_A, tm), lambda b, t: (b, 0, 0))] * 2
        + [pl.BlockSpec((None, W_B, tm), lambda b, t: (b, 0, t))] * 2,
        out_shape=[jax.ShapeDtypeStruct((rows, w), BF16) for w in widths]
        + [jax.ShapeDtypeStruct((batch, W_A, tm), F32)] * 2
        + [jax.ShapeDtypeStruct((batch, W_B, seq), F32)] * 2,
        compiler_params=pltpu.CompilerParams(
            dimension_semantics=("arbitrary", "arbitrary"), vmem_limit_bytes=VMEM_LIMIT),
        name="in_projection_prompt",
    )(x2d, w_in_bf)


def _inproj_sample_kernel(x_ref, w_ref, qa, ka, va, qb, kb, vb, qm, ka32, va32, kb32, vb32):
    for y, out, out32 in zip(_project(x_ref, w_ref), (qa, ka, va, qb, kb, vb, qm),
                             (None, ka32, va32, None, kb32, vb32, None)):
        out[...] = y.astype(BF16)
        if out32 is not None:
            out32[...] = y


def _in_projection_sample(x2d, w_in_bf):
    rows, d = x2d.shape
    tm = min(ROW_TILE, rows)
    assert rows % tm == 0
    d_in = w_in_bf.shape[1]
    widths = [W_A, W_A, W_A, W_B, W_B, W_B, W_M, W_A, W_A, W_B, W_B]
    dtypes = [BF16] * 7 + [F32] * 4
    return pl.pallas_call(
        _inproj_sample_kernel,
        grid=(rows // tm,),
        in_specs=[pl.BlockSpec((tm, d), lambda i: (i, 0)),
                  pl.BlockSpec((d, d_in), lambda i: (0, 0))],
        out_specs=[pl.BlockSpec((tm, w), lambda i: (i, 0)) for w in widths],
        out_shape=[jax.ShapeDtypeStruct((rows, w), dt) for w, dt in zip(widths, dtypes)],
        compiler_params=pltpu.CompilerParams(
            dimension_semantics=("arbitrary",), vmem_limit_bytes=VMEM_LIMIT),
        name="in_projection_sample",
    )(x2d, w_in_bf)


def _memkv_kernel(m_ref, w_ref, k_t, v_t, kbf, vbf):
    y = _dot(m_ref[...].astype(BF16), w_ref[...])
    k = y[:, :W_M]
    v = y[:, W_M:]
    k_t[...] = k.T
    v_t[...] = v.T
    kbf[...] = k.astype(BF16)
    vbf[...] = v.astype(BF16)


def _memory_kv(mem2d, w_bf, batch, n_mem):
    rows, d = mem2d.shape
    return pl.pallas_call(
        _memkv_kernel,
        grid=(batch,),
        in_specs=[pl.BlockSpec((n_mem, d), lambda b: (b, 0)),
                  pl.BlockSpec((d, 2 * W_M), lambda b: (0, 0))],
        out_specs=[pl.BlockSpec((None, W_M, n_mem), lambda b: (b, 0, 0))] * 2
        + [pl.BlockSpec((n_mem, W_M), lambda b: (b, 0))] * 2,
        out_shape=[jax.ShapeDtypeStruct((batch, W_M, n_mem), F32)] * 2
        + [jax.ShapeDtypeStruct((rows, W_M), BF16)] * 2,
        compiler_params=pltpu.CompilerParams(
            dimension_semantics=("arbitrary",), vmem_limit_bytes=VMEM_LIMIT),
        name="memory_kv",
    )(mem2d, w_bf)


def _softmax_pv(segments):
    m = None
    for s, _, _ in segments:
        ms = jnp.max(s, axis=-1, keepdims=True)
        m = ms if m is None else jnp.maximum(m, ms)
    acc = None
    l = None
    for s, v, v_is_t in segments:
        e = jnp.exp(s - m)
        ls = jnp.sum(e, axis=-1, keepdims=True)
        pv = _dot_nt(e.astype(BF16), v) if v_is_t else _dot(e.astype(BF16), v)
        acc = pv if acc is None else acc + pv
        l = ls if l is None else l + ls
    return acc, l


def _memory_attention_tile(q, mk, mv, masks, kv_is_t):
    t = q.shape[0]
    qs = _stack_heads(q, masks)
    s = _dot(qs, mk) if kv_is_t else _dot_nt(qs, mk)
    acc, l = _softmax_pv([(s, mv, kv_is_t)])
    return _unstack_heads(acc / l, masks, t)


def _toeplitz(vals, n_rows, n_cols):
    h, length = vals.shape
    assert length == n_rows + n_cols - 1
    flat = jnp.tile(vals, (1, n_rows + 1))[:, :n_rows * (length + 1)]
    return flat.reshape(h, n_rows, length + 1)[:, :, :n_cols]


def _rel_bias(table, n_rows, n_cols, rel00):
    q = jnp.arange(n_rows + n_cols - 1)
    rel = rel00 - (n_cols - 1) + q
    vals = table[:, jnp.clip(rel, -REL_CLIP, REL_CLIP) + REL_CLIP].astype(F32)
    return jnp.flip(_toeplitz(vals, n_rows, n_cols), axis=2)


def _attn_am_prompt_kernel(qa_ref, kp_ref, kc_ref, vp_ref, vc_ref, qm_ref, mk_ref, mv_ref,
                           bias_ref, oa_ref, om_ref, kwin, vwin):
    t = pl.program_id(1)
    tq = qa_ref.shape[0]
    masks = _head_masks()
    kwin[0:BAND, :] = kp_ref[...]
    kwin[BAND:BAND + tq, :] = kc_ref[...]
    vwin[0:BAND, :] = vp_ref[...]
    vwin[BAND:BAND + tq, :] = vc_ref[...]
    col = lax.broadcasted_iota(jnp.int32, (1, PAIR_WIN), 1)
    for p in range(tq // PAIR):
        r0 = p * PAIR
        q = qa_ref[r0:r0 + PAIR, :]
        qs = _stack_heads(q, masks)
        s = _dot_nt(qs, kwin[r0:r0 + PAIR_WIN, :])
        s = s + bias_ref[...]
        before_start = jnp.logical_and(t == 0, col < BAND - r0)
        s = jnp.where(before_start, NEG_INF, s)
        acc, l = _softmax_pv([(s, vwin[r0:r0 + PAIR_WIN, :], False)])
        oa_ref[r0:r0 + PAIR, :] = _unstack_heads(acc / l, masks, PAIR).astype(BF16)
    mk = mk_ref[...]
    mv = mv_ref[...]
    for p in range(tq // PAIR):
        r0 = p * PAIR
        o = _memory_attention_tile(qm_ref[r0:r0 + PAIR, :], mk, mv, masks, False)
        om_ref[r0:r0 + PAIR, :] = o.astype(BF16)


def _pair_bias(table):
    qc = jnp.arange(PAIR)[:, None] // CHUNK
    kc = jnp.arange(PAIR_WIN)[None, :] // CHUNK
    visible = jnp.logical_and(kc >= qc, kc <= qc + LEFT_CHUNKS)
    bias = jnp.where(visible[None], _rel_bias(table, PAIR, PAIR_WIN, BAND), NEG_INF)
    return bias.reshape(H_A * PAIR, PAIR_WIN)


def _attn_am_prompt(qa, ka, va, qm, mk, mv, bias, batch, seq, n_mem):
    tq = BAND
    assert seq % tq == 0
    nt = seq // tq
    cur = lambda b, t: (b * nt + t, 0)
    prev = lambda b, t: (b * nt + jnp.maximum(t - 1, 0), 0)
    tile = lambda im: pl.BlockSpec((tq, GROUP_W), im)
    return pl.pallas_call(
        _attn_am_prompt_kernel,
        grid=(batch, nt),
        in_specs=[tile(cur), tile(prev), tile(cur), tile(prev), tile(cur), tile(cur),
                  pl.BlockSpec((n_mem, GROUP_W), lambda b, t: (b, 0)),
                  pl.BlockSpec((n_mem, GROUP_W), lambda b, t: (b, 0)),
                  pl.BlockSpec(bias.shape, lambda b, t: (0, 0))],
        out_specs=[tile(cur), tile(cur)],
        out_shape=[jax.ShapeDtypeStruct((batch * seq, GROUP_W), BF16)] * 2,
        scratch_shapes=[pltpu.VMEM((BAND + tq, GROUP_W), BF16)] * 2,
        compiler_params=pltpu.CompilerParams(
            dimension_semantics=("arbitrary", "arbitrary"), vmem_limit_bytes=VMEM_LIMIT),
        name="attn_am_prompt",
    )(qa, ka, ka, va, va, qm, mk, mv, bias)


class _SbScratch(NamedTuple):
    qs: object
    z: object
    e: object
    c0: object
    acc: object
    carry: object


def _sb_scratch_shapes(rows):
    return [pltpu.VMEM((rows, GROUP_W), BF16), pltpu.VMEM((rows, SB_KB), F32),
            pltpu.VMEM((rows, SB_KB), F32), pltpu.VMEM((rows, 1), F32),
            pltpu.VMEM((rows, GROUP_W), F32), pltpu.VMEM((rows, 1), F32)]


def _sb_reset(s):
    s.e[...] = jnp.full(s.e.shape, MASKED_SCORE, F32)
    s.c0[...] = jnp.zeros_like(s.c0)
    s.acc[...] = jnp.zeros_like(s.acc)
    s.carry[...] = jnp.zeros_like(s.carry)


def _sb_scores(rows, s, k_blk, mask, k_is_t=False):
    q = s.qs[rows, :]
    z = _dot(q, k_blk) if k_is_t else _dot_nt(q, k_blk)
    if mask is not None:
        z = jnp.where(mask, z, MASKED_SCORE)
    s.z[rows, :] = z


def _sb_cumsum(rows, s, tmat):
    z = s.z[rows, :]
    sign = jnp.uint32(0x80000000)
    neg_abs = lax.bitcast_convert_type(lax.bitcast_convert_type(z, jnp.uint32) | sign, F32)
    sp = jnp.maximum(z, 0.0) + jnp.log(1.0 + jnp.exp2(neg_abs)) * LOG2_E
    hi = sp.astype(BF16)
    lo = (sp - hi.astype(F32)).astype(BF16)
    incl = _dot(hi, tmat) + _dot(lo, tmat)
    s.e[rows, :] = z - incl
    s.c0[rows, :] = incl[:, 0:1]


def _sb_apply(rows, s, v_blk, v_is_t=False):
    w = jnp.exp2(s.e[rows, :] - s.carry[rows, :]).astype(BF16)
    s.acc[rows, :] += _dot_nt(w, v_blk) if v_is_t else _dot(w, v_blk)
    s.carry[rows, :] += s.c0[rows, :]


def _sb_done(s):
    return (jnp.min(s.carry[...]) >= SB_DONE).astype(jnp.int32)


def _row_chunks(n, parts):
    step = n // parts
    return [slice(c * step, (c + 1) * step) for c in range(parts)]


def _sb_prompt_kernel(q_ref, k_ref, v_ref, t_ref, o_ref, *scratch):
    i = pl.program_id(2)
    s = _SbScratch(*scratch)
    masks = _head_masks()
    tmat = t_ref[...]
    s.qs[...] = _stack_heads(q_ref[...], masks)
    _sb_reset(s)
    chunks = _row_chunks(s.qs.shape[0], SB_CHAINS)

    def key_block(ref, j):
        return ref[pl.ds(pl.multiple_of(j * SB_KB, SB_KB), SB_KB), :]

    k_pos = lax.broadcasted_iota(jnp.int32, (1, SB_KB), 1)
    for rows in chunks:
        n = rows.stop - rows.start
        q_pos = (rows.start + lax.broadcasted_iota(jnp.int32, (n, 1), 0)) & (SB_TQ - 1)
        _sb_scores(rows, s, key_block(k_ref, i), k_pos < q_pos)

    def unfinished(state):
        t, done = state
        return jnp.logical_and(t <= i, done == 0)

    def step(state):
        t, _ = state
        v_blk = key_block(v_ref, jnp.minimum(i - t + 2, i))
        k_blk = key_block(k_ref, i - t)
        for rows in chunks:
            _sb_apply(rows, s, v_blk)
            _sb_cumsum(rows, s, tmat)
            _sb_scores(rows, s, k_blk, None)
        return t + 1, _sb_done(s)

    _, done = lax.while_loop(unfinished, step, (jnp.int32(1), jnp.int32(0)))

    @pl.when(done == 0)
    def _():
        v_blk = key_block(v_ref, jnp.minimum(1, i))
        for rows in chunks:
            _sb_apply(rows, s, v_blk)
            _sb_cumsum(rows, s, tmat)
        v_blk = key_block(v_ref, 0)
        for rows in chunks:
            _sb_apply(rows, s, v_blk)

    o_ref[...] = _unstack_heads(s.acc[...], masks, SB_TQ).astype(BF16)


def _cumsum_matrix(n):
    j = jnp.arange(n)[:, None]
    s = jnp.arange(n)[None, :]
    return (j >= s).astype(BF16)


def _sb_prompt(qb, kb, vb, batch, seq):
    assert seq % SB_KB == 0 and SB_KB == SB_TQ
    nq = seq // SB_TQ
    groups = W_B // GROUP_W
    rows = HEADS_PER_GROUP * SB_TQ
    return pl.pallas_call(
        _sb_prompt_kernel,
        grid=(batch, groups, nq),
        in_specs=[pl.BlockSpec((SB_TQ, GROUP_W), lambda b, g, i: (b * nq + i, g)),
                  pl.BlockSpec((seq, GROUP_W), lambda b, g, i: (b, g)),
                  pl.BlockSpec((seq, GROUP_W), lambda b, g, i: (b, g)),
                  pl.BlockSpec((SB_KB, SB_KB), lambda b, g, i: (0, 0))],
        out_specs=pl.BlockSpec((SB_TQ, GROUP_W), lambda b, g, i: (b * nq + i, g)),
        out_shape=jax.ShapeDtypeStruct((batch * seq, W_B), BF16),
        scratch_shapes=_sb_scratch_shapes(rows),
        compiler_params=pltpu.CompilerParams(
            dimension_semantics=("arbitrary", "arbitrary", "arbitrary"), vmem_limit_bytes=VMEM_LIMIT),
        name="stick_breaking_prompt",
    )(qb, kb, vb, _cumsum_matrix(SB_KB))


def _attn_am_sample_kernel(qa_ref, kn_ref, vn_ref, ck_ref, cv_ref, qm_ref, mk_ref, mv_ref,
                           bias_c_ref, bias_n_ref, oa_ref, om_ref):
    masks = _head_masks()
    t = qa_ref.shape[0]
    qs = _stack_heads(qa_ref[...], masks)
    s_c = _dot(qs, ck_ref[...].astype(BF16)) + bias_c_ref[...]
    s_n = _dot_nt(qs, kn_ref[...]) + bias_n_ref[...]
    acc, l = _softmax_pv([(s_c, cv_ref[...].astype(BF16), True), (s_n, vn_ref[...], False)])
    oa_ref[...] = _unstack_heads(acc / l, masks, t).astype(BF16)
    o = _memory_attention_tile(qm_ref[...], mk_ref[...].astype(BF16), mv_ref[...].astype(BF16), masks, True)
    om_ref[...] = o.astype(BF16)


def _sample_bias(table, t, c):
    bias_c = _rel_bias(table, t, c, c)
    bias_n = jnp.where((jnp.arange(NEW_PAD) < t)[None, None, :], _rel_bias(table, t, NEW_PAD, 0), NEG_INF)
    return bias_c.reshape(H_A * t, c), bias_n.reshape(H_A * t, NEW_PAD)


def _pad_new(a, nb, t):
    w = a.shape[1]
    return jnp.pad(a.reshape(nb, t, w), ((0, 0), (0, NEW_PAD - t), (0, 0))).reshape(nb * NEW_PAD, w)


def _attn_am_sample(qa, kn_pad, vn_pad, ck_t, cv_t, qm, cmk_t, cmv_t, bias_c, bias_n, nb, t):
    c = ck_t.shape[2]
    n_mem = cmk_t.shape[2]
    row = lambda w: pl.BlockSpec((t, w), lambda b: (b, 0))
    return pl.pallas_call(
        _attn_am_sample_kernel,
        grid=(nb,),
        in_specs=[row(GROUP_W),
                  pl.BlockSpec((NEW_PAD, GROUP_W), lambda b: (b, 0)),
                  pl.BlockSpec((NEW_PAD, GROUP_W), lambda b: (b, 0)),
                  pl.BlockSpec((None, GROUP_W, c), lambda b: (b, 0, 0)),
                  pl.BlockSpec((None, GROUP_W, c), lambda b: (b, 0, 0)),
                  row(GROUP_W),
                  pl.BlockSpec((None, GROUP_W, n_mem), lambda b: (b, 0, 0)),
                  pl.BlockSpec((None, GROUP_W, n_mem), lambda b: (b, 0, 0)),
                  pl.BlockSpec(bias_c.shape, lambda b: (0, 0)),
                  pl.BlockSpec(bias_n.shape, lambda b: (0, 0))],
        out_specs=[row(GROUP_W), row(GROUP_W)],
        out_shape=[jax.ShapeDtypeStruct((nb * t, GROUP_W), BF16)] * 2,
        compiler_params=pltpu.CompilerParams(
            dimension_semantics=("arbitrary",), vmem_limit_bytes=VMEM_LIMIT),
        name="attn_am_sample",
    )(qa, kn_pad, vn_pad, ck_t, cv_t, qm, cmk_t, cmv_t, bias_c, bias_n)


def _sb_sample_kernel(q_ref, kn_ref, vn_ref, ck_ref, cv_ref, t_ref, o_ref, *scratch):
    s = _SbScratch(*scratch)
    masks = _head_masks()
    t = q_ref.shape[0]
    tmat = t_ref[...]
    s.qs[...] = _stack_heads(q_ref[...], masks)
    _sb_reset(s)
    n = s.qs.shape[0]
    rows = slice(0, n)
    nblk = ck_ref.shape[1] // SB_KB

    def cache_block(ref, j):
        return ref[:, pl.ds(pl.multiple_of(j * SB_KB, SB_KB), SB_KB)].astype(BF16)

    q_idx = lax.broadcasted_iota(jnp.int32, (n, 1), 0) & (t - 1)
    k_idx = lax.broadcasted_iota(jnp.int32, (1, NEW_PAD), 1)
    _sb_scores(rows, s, kn_ref[...], k_idx < q_idx)
    _sb_cumsum(rows, s, tmat)
    _sb_apply(rows, s, vn_ref[...])

    def unfinished(state):
        j, done = state
        return jnp.logical_and(j >= 0, done == 0)

    def step(state):
        j, _ = state
        _sb_scores(rows, s, cache_block(ck_ref, j), None, k_is_t=True)
        _sb_cumsum(rows, s, tmat)
        _sb_apply(rows, s, cache_block(cv_ref, j), v_is_t=True)
        return j - 1, _sb_done(s)

    lax.while_loop(unfinished, step, (jnp.int32(nblk - 1), _sb_done(s)))
    o_ref[...] = _unstack_heads(s.acc[...], masks, t).astype(BF16)


def _sb_sample(qb, kn_pad, vn_pad, ck_t, cv_t, nb, t):
    past = ck_t.shape[2]
    assert past % SB_KB == 0 and (t & (t - 1)) == 0 and t <= NEW_PAD
    groups = W_B // GROUP_W
    rows = HEADS_PER_GROUP * t
    return pl.pallas_call(
        _sb_sample_kernel,
        grid=(nb, groups),
        in_specs=[pl.BlockSpec((t, GROUP_W), lambda b, g: (b, g)),
                  pl.BlockSpec((NEW_PAD, GROUP_W), lambda b, g: (b, g)),
                  pl.BlockSpec((NEW_PAD, GROUP_W), lambda b, g: (b, g)),
                  pl.BlockSpec((None, GROUP_W, past), lambda b, g: (b, g, 0)),
                  pl.BlockSpec((None, GROUP_W, past), lambda b, g: (b, g, 0)),
                  pl.BlockSpec((SB_KB, SB_KB), lambda b, g: (0, 0))],
        out_specs=pl.BlockSpec((t, GROUP_W), lambda b, g: (b, g)),
        out_shape=jax.ShapeDtypeStruct((nb * t, W_B), BF16),
        scratch_shapes=_sb_scratch_shapes(rows),
        compiler_params=pltpu.CompilerParams(
            dimension_semantics=("arbitrary", "arbitrary"), vmem_limit_bytes=VMEM_LIMIT),
        name="stick_breaking_sample",
    )(qb, kn_pad, vn_pad, ck_t, cv_t, _cumsum_matrix(SB_KB))


def _merge_kernel(alpha, x_ref, oa_ref, ob_ref, om_ref, wg_ref, bg_ref, wpa_ref, wpb_ref, wpm_ref,
                  wo_ref, g_ref, b_ref, h_ref):
    x = x_ref[...]
    xb = x.astype(BF16)
    d = x.shape[1]

    def gate(k):
        return _sigmoid(_dot(xb, wg_ref[:, k * d:(k + 1) * d]) + bg_ref[:, k * d:(k + 1) * d])

    hm = gate(0) * _dot(oa_ref[...], wpa_ref[...])
    hm = hm + gate(1) * _dot(ob_ref[...], wpb_ref[...])
    hm = hm + gate(2) * _dot(om_ref[...], wpm_ref[...])
    mix = _dot(hm.astype(BF16), wo_ref[...])
    h_ref[...] = _layer_norm(alpha * x + mix, g_ref[...], b_ref[...])


def _merge(x2d, oa, ob, om, wg, bg, wpa, wpb, wpm, wo, g1, b1, alpha):
    rows, d = x2d.shape
    tm = min(ROW_TILE, rows)
    assert rows % tm == 0
    row = lambda w: pl.BlockSpec((tm, w), lambda i: (i, 0))
    full = lambda a: pl.BlockSpec(a.shape, lambda i: (0, 0))
    return pl.pallas_call(
        functools.partial(_merge_kernel, alpha),
        grid=(rows // tm,),
        in_specs=[row(d), row(W_A), row(W_B), row(W_M), full(wg), full(bg), full(wpa), full(wpb),
                  full(wpm), full(wo), full(g1), full(b1)],
        out_specs=row(d),
        out_shape=jax.ShapeDtypeStruct((rows, d), F32),
        compiler_params=pltpu.CompilerParams(
            dimension_semantics=("arbitrary",), vmem_limit_bytes=VMEM_LIMIT),
        name="merge_ln1",
    )(x2d, oa, ob, om, wg, bg, wpa, wpb, wpm, wo, g1, b1)


def _causal_conv3(u, hist, w, b):
    t = u.shape[0]
    r = lax.broadcasted_iota(jnp.int32, (HIST_ROWS, 1), 0)
    u1 = pltpu.roll(u, 1, 0)
    u2 = pltpu.roll(u, 2, 0)
    h1 = hist[HIST_ROWS - 1:HIST_ROWS, :]
    h2 = hist[HIST_ROWS - 2:HIST_ROWS - 1, :]
    head1 = jnp.where(r == 0, h1, u1[0:HIST_ROWS])
    head2 = jnp.where(r == 0, h2, jnp.where(r == 1, h1, u2[0:HIST_ROWS]))
    if t > HIST_ROWS:
        u1 = jnp.concatenate([head1, u1[HIST_ROWS:]], axis=0)
        u2 = jnp.concatenate([head2, u2[HIST_ROWS:]], axis=0)
    else:
        u1, u2 = head1, head2
    return w[0:1, :] * u2 + w[1:2, :] * u1 + w[2:3, :] * u + b


def _ffn_prompt_kernel(alpha, d_ff, h_ref, wup_ref, cw_ref, cb_ref, wdn_ref, g_ref, b_ref,
                       y_ref, tail_ref, hist_ref):
    t = pl.program_id(1)

    @pl.when(t == 0)
    def _():
        hist_ref[...] = jnp.zeros_like(hist_ref)

    h = h_ref[...]
    hb = h.astype(BF16)
    tm = h.shape[0]
    acc = jnp.zeros(h.shape, F32)
    for c in range(d_ff // FF_CHUNK):
        gs = slice(c * FF_CHUNK, (c + 1) * FF_CHUNK)
        vs = slice(d_ff + c * FF_CHUNK, d_ff + (c + 1) * FF_CHUNK)
        ug = _dot(hb, wup_ref[:, gs])
        uv = _dot(hb, wup_ref[:, vs])
        cg = _causal_conv3(ug, hist_ref[:, gs], cw_ref[:, gs], cb_ref[:, gs])
        cv = _causal_conv3(uv, hist_ref[:, vs], cw_ref[:, vs], cb_ref[:, vs])
        a = cg * _sigmoid(cg) * cv
        acc = acc + _dot(a.astype(BF16), wdn_ref[gs, :])
        hist_ref[:, gs] = ug[tm - HIST_ROWS:tm]
        hist_ref[:, vs] = uv[tm - HIST_ROWS:tm]
    y_ref[...] = _layer_norm(alpha * h + acc, g_ref[...], b_ref[...])
    tail_ref[...] = hist_ref[...]


def _ffn_prompt(h2d, wup, cw, cb, wdn, g2, b2, alpha, batch, seq):
    d = h2d.shape[1]
    d_ff = wdn.shape[0]
    assert d_ff % FF_CHUNK == 0
    tm = min(ROW_TILE, seq)
    assert seq % tm == 0
    nt = seq // tm
    full = lambda a: pl.BlockSpec(a.shape, lambda b, t: (0, 0))
    return pl.pallas_call(
        functools.partial(_ffn_prompt_kernel, alpha, d_ff),
        grid=(batch, nt),
        in_specs=[pl.BlockSpec((tm, d), lambda b, t: (b * nt + t, 0)),
                  full(wup), full(cw), full(cb), full(wdn), full(g2), full(b2)],
        out_specs=[pl.BlockSpec((tm, d), lambda b, t: (b * nt + t, 0)),
                   pl.BlockSpec((None, HIST_ROWS, 2 * d_ff), lambda b, t: (b, 0, 0))],
        out_shape=[jax.ShapeDtypeStruct((batch * seq, d), F32),
                   jax.ShapeDtypeStruct((batch, HIST_ROWS, 2 * d_ff), F32)],
        scratch_shapes=[pltpu.VMEM((HIST_ROWS, 2 * d_ff), F32)],
        compiler_params=pltpu.CompilerParams(
            dimension_semantics=("arbitrary", "arbitrary"), vmem_limit_bytes=VMEM_LIMIT),
        name="conv_ffn_prompt",
    )(h2d, wup, cw, cb, wdn, g2, b2)


def _ffn_sample_kernel(alpha, t, h_ref, sg_ref, sv_ref, wg_ref, wv_ref, cwg_ref, cwv_ref, cbg_ref,
                       cbv_ref, wdn_ref, g_ref, b_ref, y_ref, ug_ref, uv_ref, a_ref, acc_ref):
    c = pl.program_id(0)
    h = h_ref[...]
    hb = h.astype(BF16)
    nb = h.shape[0] // t
    ug_ref[...] = _dot(hb, wg_ref[...])
    uv_ref[...] = _dot(hb, wv_ref[...])

    def body(b, _):
        rows = pl.ds(pl.multiple_of(b * t, t), t)
        cg = _causal_conv3(ug_ref[rows, :], sg_ref[b], cwg_ref[...], cbg_ref[...])
        cv = _causal_conv3(uv_ref[rows, :], sv_ref[b], cwv_ref[...], cbv_ref[...])
        a_ref[rows, :] = (cg * _sigmoid(cg) * cv).astype(BF16)
        return 0

    lax.fori_loop(0, nb, body, 0)
    part = _dot(a_ref[...], wdn_ref[...])

    @pl.when(c == 0)
    def _():
        acc_ref[...] = part

    @pl.when(c > 0)
    def _():
        acc_ref[...] += part

    @pl.when(c == pl.num_programs(0) - 1)
    def _():
        y_ref[...] = _layer_norm(alpha * h + acc_ref[...], g_ref[...], b_ref[...])


def _ffn_sample(h2d, state_pad, wup, cw, cb, wdn, g2, b2, alpha, t):
    rows, d = h2d.shape
    d_ff = wdn.shape[0]
    nb = rows // t
    assert d_ff % FF_CHUNK == 0 and t % 16 == 0
    nc = d_ff // FF_CHUNK
    full2 = lambda a: pl.BlockSpec(a.shape, lambda c: (0, 0))
    gate_cols = lambda r: pl.BlockSpec((r, FF_CHUNK), lambda c: (0, c))
    val_cols = lambda r: pl.BlockSpec((r, FF_CHUNK), lambda c: (0, nc + c))
    return pl.pallas_call(
        functools.partial(_ffn_sample_kernel, alpha, t),
        grid=(nc,),
        in_specs=[full2(h2d),
                  pl.BlockSpec((nb, HIST_ROWS, FF_CHUNK), lambda c: (0, 0, c)),
                  pl.BlockSpec((nb, HIST_ROWS, FF_CHUNK), lambda c: (0, 0, nc + c)),
                  gate_cols(d), val_cols(d), gate_cols(CONV_W), val_cols(CONV_W),
                  gate_cols(1), val_cols(1),
                  pl.BlockSpec((FF_CHUNK, d), lambda c: (c, 0)), full2(g2), full2(b2)],
        out_specs=[pl.BlockSpec((rows, d), lambda c: (0, 0)),
                   pl.BlockSpec((rows, FF_CHUNK), lambda c: (0, c)),
                   pl.BlockSpec((rows, FF_CHUNK), lambda c: (0, c))],
        out_shape=[jax.ShapeDtypeStruct((rows, d), F32),
                   jax.ShapeDtypeStruct((rows, d_ff), F32),
                   jax.ShapeDtypeStruct((rows, d_ff), F32)],
        scratch_shapes=[pltpu.VMEM((rows, FF_CHUNK), BF16), pltpu.VMEM((rows, d), F32)],
        compiler_params=pltpu.CompilerParams(
            dimension_semantics=("arbitrary",), vmem_limit_bytes=VMEM_LIMIT),
        name="conv_ffn_sample",
    )(h2d, state_pad, state_pad, wup, wup, cw, cw, cb, cb, wdn, g2, b2)


def _feature_major(cache):
    n, time, heads, dim = cache.shape
    return jnp.transpose(cache, (0, 2, 3, 1)).reshape(n, heads * dim, time)


def _time_major(cache_t, heads):
    n, width, time = cache_t.shape
    return jnp.transpose(cache_t.reshape(n, heads, width // heads, time), (0, 3, 1, 2))


def kernel(x_prompt, x_sample, cache_a_k, cache_a_v, cache_b_k, cache_b_v, cache_mem_k, cache_mem_v, state_ffn_conv, mem_prompt, w_in, rel_bias, w_mem_kv, w_pa, w_pb, w_pm, w_gate, b_gate, w_o, ln1_g, ln1_b, w_up, conv_w, conv_b, w_down, ln2_g, ln2_b):
    depth = w_in.shape[0]
    batch, seq, d = x_prompt.shape
    nb, t, _ = x_sample.shape
    n_mem = mem_prompt.shape[1]
    a_cache = cache_a_k.shape[2]
    d_ff = w_down.shape[1]
    alpha = (2 * depth) ** 0.25
    assert conv_w.shape[1] == CONV_W and t >= CONV_W - 1
    assert min(BAND, seq) == ROW_TILE

    yp = x_prompt.reshape(batch * seq, d)
    ys = x_sample.reshape(nb * t, d)
    outs = {k: [] for k in ("pak", "pav", "pbk", "pbv", "pmk", "pmv", "pconv",
                            "sak", "sav", "sbk", "sbv", "sconv")}
    for l in range(depth):
        bf = lambda a: a[l].astype(BF16)
        w_in_l, w_mem_l = bf(w_in), bf(w_mem_kv)
        wg, wpa, wpb, wpm, wo = bf(w_gate), bf(w_pa), bf(w_pb), bf(w_pm), bf(w_o)
        wup, wdn = bf(w_up), bf(w_down)
        row = lambda a: a[l][None, :]
        bg, g1, b1, g2, b2, cb = row(b_gate), row(ln1_g), row(ln1_b), row(ln2_g), row(ln2_b), row(conv_b)
        cw = conv_w[l]

        qa, ka, va, qb, kb, vb, qm, ka_t, va_t, kb_t, vb_t = _in_projection_prompt(yp, w_in_l, batch, seq)
        mk_t, mv_t, mk, mv = _memory_kv(mem_prompt.reshape(batch * n_mem, d), w_mem_l, batch, n_mem)
        oa, om = _attn_am_prompt(qa, ka, va, qm, mk, mv, _pair_bias(rel_bias[l]), batch, seq, n_mem)
        ob = _sb_prompt(qb, kb, vb, batch, seq)
        h = _merge(yp, oa, ob, om, wg, bg, wpa, wpb, wpm, wo, g1, b1, alpha)
        yp, tail = _ffn_prompt(h, wup, cw, cb, wdn, g2, b2, alpha, batch, seq)
        outs["pak"].append(_time_major(ka_t, H_A))
        outs["pav"].append(_time_major(va_t, H_A))
        outs["pbk"].append(_time_major(kb_t, H_B))
        outs["pbv"].append(_time_major(vb_t, H_B))
        outs["pmk"].append(_time_major(mk_t, H_M))
        outs["pmv"].append(_time_major(mv_t, H_M))
        outs["pconv"].append(tail[:, HIST_ROWS - (CONV_W - 1):])

        qa, ka, va, qb, kb, vb, qm, ka32, va32, kb32, vb32 = _in_projection_sample(ys, w_in_l)
        bias_c, bias_n = _sample_bias(rel_bias[l], t, a_cache)
        oa, om = _attn_am_sample(
            qa, _pad_new(ka, nb, t), _pad_new(va, nb, t),
            _feature_major(cache_a_k[l]), _feature_major(cache_a_v[l]), qm,
            _feature_major(cache_mem_k[l]), _feature_major(cache_mem_v[l]), bias_c, bias_n, nb, t)
        ob = _sb_sample(qb, _pad_new(kb, nb, t), _pad_new(vb, nb, t),
                        _feature_major(cache_b_k[l]), _feature_major(cache_b_v[l]), nb, t)
        h = _merge(ys, oa, ob, om, wg, bg, wpa, wpb, wpm, wo, g1, b1, alpha)
        state_pad = jnp.pad(state_ffn_conv[l], ((0, 0), (HIST_ROWS - (CONV_W - 1), 0), (0, 0)))
        ys, ug, uv = _ffn_sample(h, state_pad, wup, cw, cb, wdn, g2, b2, alpha, t)
        tail_rows = lambda u: u.reshape(nb, t, d_ff)[:, -(CONV_W - 1):]
        outs["sak"].append(ka32.reshape(nb, t, H_A, HEAD_DIM))
        outs["sav"].append(va32.reshape(nb, t, H_A, HEAD_DIM))
        outs["sbk"].append(kb32.reshape(nb, t, H_B, HEAD_DIM))
        outs["sbv"].append(vb32.reshape(nb, t, H_B, HEAD_DIM))
        outs["sconv"].append(jnp.concatenate([tail_rows(ug), tail_rows(uv)], axis=-1))

    st = lambda k: jnp.stack(outs[k])
    return (yp.reshape(batch, seq, d), ys.reshape(nb, t, d),
            st("pak"), st("pav"), st("pbk"), st("pbv"), st("pmk"), st("pmv"), st("pconv"),
            st("sak"), st("sav"), st("sbk"), st("sbv"), st("sconv"))
```
